```python
import math
import jax, jax.numpy as jnp
from jax import lax
import numpy as np

D_MODEL = 2048
BATCH = 4
SEQ = 8192
DEPTH = 4
DEC_BATCH = 32
DEC_SEQ = 32
PAST_LEN = 4096

CHUNK = 64
EPS = 1e-6
NEG_BIG = -1e30
A_HEADS = 4
A_DQK = 256
A_DV = 512
CONV_W = 4
A_QK = 2 * A_HEADS * A_DQK
A_V = A_HEADS * A_DV
B_HEADS = 16
B_DK = 128
B_DV = 128
B_K = B_HEADS * B_DK
B_V = B_HEADS * B_DV
N_MEM = 256
M_HEADS = 4
M_HD = 512
M_W = M_HEADS * M_HD
D_FF = 5632
N_IN = A_QK + 2 * A_V + 2 * A_HEADS + 2 * B_K + 2 * B_V + M_W + 3 * D_MODEL

kernel_name = "hybrid_mlstm_hgrn2_memxattn_stream_step"


def rmsnorm(x, g):
    xf = x.astype(jnp.float32)
    y = xf * lax.rsqrt(jnp.mean(xf * xf, axis=-1, keepdims=True) + EPS)
    return (y * g.astype(jnp.float32)).astype(x.dtype)


def to_blocks(a, L):
    b, t = a.shape[:2]
    a = a.reshape((b, t // L, L) + a.shape[2:])
    return jnp.moveaxis(a, 1, 0)


def from_blocks(a):
    a = jnp.moveaxis(a, 0, 1)
    return a.reshape((a.shape[0], a.shape[1] * a.shape[2]) + a.shape[3:])


def causal_conv(u, buf, w, b):
    t = u.shape[1]
    up = jnp.concatenate([buf.astype(u.dtype), u], axis=1)
    out = b
    for j in range(CONV_W):
        out = out + w[j] * up[:, j:j + t]
    return out, up[:, t:]


def mlstm_scan(q, k, v, log_i, log_f, c0, n0, m0):
    t = q.shape[1]
    L = min(CHUNK, t)
    causal = jnp.tril(jnp.ones((L, L), dtype=bool))

    def step(carry, blk):
        c, n, m = carry
        qb, kb, vb, ib, fb = blk
        fcum = jnp.moveaxis(jnp.cumsum(fb, axis=1), 1, 2)
        ib = jnp.moveaxis(ib, 1, 2)
        d = fcum[..., :, None] - fcum[..., None, :] + ib[..., None, :]
        d = jnp.where(causal, d, NEG_BIG)
        inter = fcum + m[..., None]
        m_t = jnp.maximum(inter, jnp.max(d, axis=-1))
        w_intra = jnp.exp(d - m_t[..., None])
        w_state = jnp.exp(inter - m_t)
        s = jnp.einsum('blhd,bshd->bhls', qb, kb) * w_intra
        num = (jnp.einsum('bhls,bshe->blhe', s, vb)
               + jnp.moveaxis(w_state, 1, 2)[..., None] * jnp.einsum('blhd,bhde->blhe', qb, c))
        den = jnp.sum(s, axis=-1) + w_state * jnp.einsum('blhd,bhd->bhl', qb, n)
        den = jnp.maximum(jnp.abs(den), jnp.exp(-m_t))
        h = num / jnp.moveaxis(den, 1, 2)[..., None]
        m_new = m_t[..., -1]
        decay = jnp.exp(fcum[..., -1] + m - m_new)
        w_end = jnp.exp(fcum[..., -1:] - fcum + ib - m_new[..., None])
        c_new = decay[..., None, None] * c + jnp.einsum('bhs,bshd,bshe->bhde', w_end, kb, vb)
        n_new = decay[..., None] * n + jnp.einsum('bhs,bshd->bhd', w_end, kb)
        return (c_new, n_new, m_new), h

    xs = tuple(to_blocks(a, L) for a in (q, k, v, log_i, log_f))
    (c, n, m), h = lax.scan(step, (c0, n0, m0), xs)
    return from_blocks(h), c, n, m


def hgrn2_scan(q, k, v, log_f, s0):
    t = q.shape[1]
    L = min(CHUNK, t)
    causal = jnp.tril(jnp.ones((L, L), dtype=bool))[None, :, :, None, None]

    def step(s, blk):
        qb, kb, vb, fb = blk
        g = jnp.cumsum(fb, axis=1)
        diff = jnp.where(causal, g[:, :, None] - g[:, None, :], NEG_BIG)
        a = jnp.einsum('bthd,bshd,btshd->bhts', qb, kb, jnp.exp(diff))
        o = (jnp.einsum('bhts,bshe->bthe', a, vb)
             + jnp.einsum('bthd,bhde->bthe', qb * jnp.exp(g), s))
        g_end = g[:, -1]
        s_new = (jnp.exp(g_end)[..., None] * s
                 + jnp.einsum('bshd,bshe->bhde', kb * jnp.exp(g_end[:, None] - g), vb))
        return s_new, o

    xs = tuple(to_blocks(a, L) for a in (q, k, v, log_f))
    s, o = lax.scan(step, s0, xs)
    return from_blocks(o), s


def mem_attention(q, mk, mv):
    sc = jnp.einsum('bthd,bnhd->bhtn', q, mk).astype(jnp.float32) * (M_HD ** -0.5)
    p = jax.nn.softmax(sc, axis=-1).astype(mv.dtype)
    return jnp.einsum('bhtn,bnhd->bthd', p, mv)


def trunk(x, mem_k, mem_v, conv0, c0, n0, m0, s0, lb,
          g_mix, w_in, conv_w, conv_b, b_igate, b_fgate, g_mlstm, g_hgrn,
          w_proj_a, w_proj_b, w_proj_m, w_out, g_ffn, w_ffn_in, w_ffn_out, g_final):
    f32 = jnp.float32
    dt = x.dtype
    bn, t, _ = x.shape
    sizes = [A_QK, A_V, A_V, 2 * A_HEADS, B_K, B_K, B_V, B_V, M_W]
    points = [int(p) for p in np.cumsum(sizes)]
    new_conv, new_c, new_n, new_m, new_s = [], [], [], [], []
    for l in range(DEPTH):
        h = rmsnorm(x, g_mix[l])
        p = h @ w_in[l]
        qk_a, v_a, o_a, if_a, q_b, f_b, i_b, gt_b, q_m, gates = jnp.split(p, points, axis=-1)

        qk_c, cbuf = causal_conv(qk_a, conv0[l], conv_w[l], conv_b[l])
        qk_c = jax.nn.silu(qk_c.astype(f32))
        qa = qk_c[..., :A_QK // 2].reshape(bn, t, A_HEADS, A_DQK)
        ka = qk_c[..., A_QK // 2:].reshape(bn, t, A_HEADS, A_DQK) * (A_DQK ** -0.5)
        va = v_a.astype(f32).reshape(bn, t, A_HEADS, A_DV)
        log_i = if_a[..., :A_HEADS].astype(f32) + b_igate[l]
        log_f = jax.nn.log_sigmoid(if_a[..., A_HEADS:].astype(f32) + b_fgate[l])
        ha, ca, na, ma = mlstm_scan(qa, ka, va, log_i, log_f,
                                    c0[l].astype(f32), n0[l].astype(f32), m0[l].astype(f32))
        ya = rmsnorm(ha, g_mlstm[l].reshape(A_HEADS, A_DV)).reshape(bn, t, A_V)
        ya = (ya * jax.nn.sigmoid(o_a.astype(f32))).astype(dt)

        lb_l = lb[l].reshape(B_HEADS, B_DK)
        z = f_b.astype(f32).reshape(bn, t, B_HEADS, B_DK)
        log_fb = jax.nn.log_sigmoid(z) + jnp.log1p(lb_l * jnp.exp(-z))
        kb = -jnp.expm1(log_fb)
        qb = jax.nn.silu(q_b.astype(f32)).reshape(bn, t, B_HEADS, B_DK)
        vb = i_b.astype(f32).reshape(bn, t, B_HEADS, B_DV)
        ob, sb = hgrn2_scan(qb, kb, vb, log_fb, s0[l].astype(f32))
        yb = rmsnorm(ob, g_hgrn[l]) * jax.nn.silu(gt_b.astype(f32)).reshape(bn, t, B_HEADS, B_DV)
        yb = yb.reshape(bn, t, B_V).astype(dt)

        ym = mem_attention(q_m.reshape(bn, t, M_HEADS, M_HD), mem_k[l].astype(dt), mem_v[l].astype(dt))
        ym = ym.reshape(bn, t, M_W)

        ga, gb, gm = jnp.split(gates, 3, axis=-1)
        mix = (jax.nn.sigmoid(ga) * (ya @ w_proj_a[l])
               + jax.nn.sigmoid(gb) * (yb @ w_proj_b[l])
               + jax.nn.sigmoid(gm) * (ym @ w_proj_m[l]))
        x = x + mix @ w_out[l]

        h2 = rmsnorm(x, g_ffn[l])
        gu = h2 @ w_ffn_in[l]
        x = x + (jax.nn.silu(gu[..., :D_FF]) * gu[..., D_FF:]) @ w_ffn_out[l]

        new_conv.append(cbuf.astype(dt))
        new_c.append(ca.astype(dt))
        new_n.append(na.astype(dt))
        new_m.append(ma.astype(dt))
        new_s.append(sb.astype(dt))
    y = rmsnorm(x, g_final)
    return (y, jnp.stack(new_conv), jnp.stack(new_c), jnp.stack(new_n),
            jnp.stack(new_m), jnp.stack(new_s))


def setup_inputs(seed: int = 0) -> dict:
    key = jax.random.key(seed)
    ks = jax.random.split(key, 32)
    nrm = jax.random.normal
    f32 = jnp.float32
    D = D_MODEL
    return {
        "x_prompt": nrm(ks[0], (BATCH, SEQ, D), f32),
        "x_sample": nrm(ks[1], (DEC_BATCH, DEC_SEQ, D), f32),
        "mem_prompt": nrm(ks[2], (BATCH, N_MEM, D), f32),
        "cache_mem_k": nrm(ks[3], (DEPTH, DEC_BATCH, N_MEM, M_HEADS, M_HD), f32),
        "cache_mem_v": nrm(ks[4], (DEPTH, DEC_BATCH, N_MEM, M_HEADS, M_HD), f32),
        "state_mlstm_c": 0.1 * nrm(ks[5], (DEPTH, DEC_BATCH, A_HEADS, A_DQK, A_DV), f32),
        "state_mlstm_n": 0.1 * nrm(ks[6], (DEPTH, DEC_BATCH, A_HEADS, A_DQK), f32),
        "state_mlstm_m": nrm(ks[7], (DEPTH, DEC_BATCH, A_HEADS), f32),
        "state_mlstm_conv": nrm(ks[8], (DEPTH, DEC_BATCH, CONV_W - 1, A_QK), f32),
        "state_hgrn_s": 0.5 * nrm(ks[9], (DEPTH, DEC_BATCH, B_HEADS, B_DK, B_DV), f32),
        "g_mix": 1.0 + 0.1 * nrm(ks[10], (DEPTH, D), f32),
        "w_in": nrm(ks[11], (DEPTH, D, N_IN), f32) * D ** -0.5,
        "conv_w": nrm(ks[12], (DEPTH, CONV_W, A_QK), f32) * CONV_W ** -0.5,
        "conv_b": 0.01 * nrm(ks[13], (DEPTH, A_QK), f32),
        "b_igate": 0.1 * nrm(ks[14], (DEPTH, A_HEADS), f32),
        "b_fgate": jnp.linspace(3.0, 6.0, A_HEADS, dtype=f32)[None, :] + 0.1 * nrm(ks[15], (DEPTH, A_HEADS), f32),
        "g_mlstm": 1.0 + 0.1 * nrm(ks[16], (DEPTH, A_V), f32),
        "lb_raw": 0.1 * nrm(ks[17], (DEPTH, B_K), f32),
        "g_hgrn": 1.0 + 0.1 * nrm(ks[18], (DEPTH, B_DV), f32),
        "g_mem": 1.0 + 0.1 * nrm(ks[19], (DEPTH, D), f32),
        "w_mem_kv": nrm(ks[20], (DEPTH, D, 2 * M_W), f32) * D ** -0.5,
        "w_proj_a": nrm(ks[21], (DEPTH, A_V, D), f32) * A_V ** -0.5,
        "w_proj_b": nrm(ks[22], (DEPTH, B_V, D), f32) * B_V ** -0.5,
        "w_proj_m": nrm(ks[23], (DEPTH, M_W, D), f32) * M_W ** -0.5,
        "w_out": nrm(ks[24], (DEPTH, D, D), f32) * D ** -0.5,
        "g_ffn": 1.0 + 0.1 * nrm(ks[25], (DEPTH, D), f32),
        "w_ffn_in": nrm(ks[26], (DEPTH, D, 2 * D_FF), f32) * D ** -0.5,
        "w_ffn_out": nrm(ks[27], (DEPTH, D_FF, D), f32) * D_FF ** -0.5,
        "g_final": 1.0 + 0.1 * nrm(ks[28], (D,), f32),
    }


def reference(x_prompt, x_sample, mem_prompt, cache_mem_k, cache_mem_v,
              state_mlstm_c, state_mlstm_n, state_mlstm_m, state_mlstm_conv, state_hgrn_s,
              g_mix, w_in, conv_w, conv_b, b_igate, b_fgate, g_mlstm, lb_raw, g_hgrn,
              g_mem, w_mem_kv, w_proj_a, w_proj_b, w_proj_m, w_out, g_ffn,
              w_ffn_in, w_ffn_out, g_final):
    f32 = jnp.float32
    dt = x_prompt.dtype
    sm = jax.nn.softmax(lb_raw.astype(f32), axis=0)
    lb = jnp.cumsum(sm, axis=0) - sm[0]
    weights = (g_mix, w_in, conv_w, conv_b, b_igate, b_fgate, g_mlstm, g_hgrn,
               w_proj_a, w_proj_b, w_proj_m, w_out, g_ffn, w_ffn_in, w_ffn_out, g_final)

    mem_n = rmsnorm(mem_prompt[None], g_mem[:, None, None, :])
    kv = jnp.einsum('lbnd,lde->lbne', mem_n, w_mem_kv)
    mem_k_p = kv[..., :M_W].reshape(DEPTH, BATCH, N_MEM, M_HEADS, M_HD)
    mem_v_p = kv[..., M_W:].reshape(DEPTH, BATCH, N_MEM, M_HEADS, M_HD)
    zc = jnp.zeros((DEPTH, BATCH, CONV_W - 1, A_QK), dt)
    zC = jnp.zeros((DEPTH, BATCH, A_HEADS, A_DQK, A_DV), f32)
    zn = jnp.zeros((DEPTH, BATCH, A_HEADS, A_DQK), f32)
    zm = jnp.zeros((DEPTH, BATCH, A_HEADS), f32)
    zs = jnp.zeros((DEPTH, BATCH, B_HEADS, B_DK, B_DV), f32)
    y_prompt, conv_p, c_p, n_p, m_p, s_p = trunk(
        x_prompt, mem_k_p, mem_v_p, zc, zC, zn, zm, zs, lb, *weights)

    y_sample, conv_s, c_s, n_s, m_s, s_s = trunk(
        x_sample, cache_mem_k, cache_mem_v, state_mlstm_conv, state_mlstm_c,
        state_mlstm_n, state_mlstm_m, state_hgrn_s, lb, *weights)

    return (y_prompt, y_sample, mem_k_p, mem_v_p, conv_p, c_p, n_p, m_p, s_p,
            conv_s, c_s, n_s, m_s, s_s)
```

```python
import functools

import jax
import jax.numpy as jnp
from jax import lax
from jax.experimental import pallas as pl
from jax.experimental.pallas import tpu as pltpu

F32 = jnp.float32
BF16 = jnp.bfloat16

EPS = 1e-6
NEG_BIG = -1e30
DEPTH = 4
CONV_W = 4
A_HEADS = 4
A_DQK = 256
A_DV = 512
B_HEADS = 16
B_DK = 128
B_DV = 128
M_HEADS = 4
M_HD = 512

LANE = 128
SUBLANE = 8
VMEM_LIMIT = 56 * 1024 * 1024
MLSTM_CHUNK = 128
HGRN_CHUNK = 64

_NT = (((1,), (1,)), ((), ()))
_TN = (((0,), (0,)), ((), ()))


def _params(*sem):
    return pltpu.CompilerParams(dimension_semantics=sem, vmem_limit_bytes=VMEM_LIMIT)


def _tile(n, pref, align):
    best = None
    t = align
    while t <= min(n, pref):
        if n % t == 0:
            best = t
        t += align
    return best if best is not None else n


def _dot(a, b):
    return jnp.dot(a, b, preferred_element_type=F32)


def _log_sigmoid(x):
    return jnp.minimum(x, 0.0) - jnp.log1p(jnp.exp(-jnp.abs(x)))


def _silu(x):
    return x * jax.nn.sigmoid(x)


def _rms(x):
    return x * lax.rsqrt(jnp.mean(x * x, axis=-1, keepdims=True) + EPS)


def _rmsnorm_kernel(x_ref, g_ref, o_ref):
    o_ref[...] = (_rms(x_ref[...]) * g_ref[...]).astype(o_ref.dtype)


def _rmsnorm(x, g, out_dtype):
    m, d = x.shape
    tm = _tile(m, 512, 16)
    return pl.pallas_call(
        _rmsnorm_kernel,
        grid=(m // tm,),
        in_specs=[pl.BlockSpec((tm, d), lambda i: (i, 0)), pl.BlockSpec((1, d), lambda i: (0, 0))],
        out_specs=pl.BlockSpec((tm, d), lambda i: (i, 0)),
        out_shape=jax.ShapeDtypeStruct((m, d), out_dtype),
        compiler_params=_params("parallel"),
        name="rmsnorm",
    )(x, g.reshape(1, d))


def _matmul_kernel(a_ref, w_ref, o_ref):
    o_ref[...] = _dot(a_ref[...], w_ref[...]).astype(o_ref.dtype)


def _matmul(a, w, layer, out_dtype, *, col_off=0, n_cols=None, tm=1024, tn=512):
    m, k = a.shape
    n_cols = w.shape[2] - col_off if n_cols is None else n_cols
    tm = _tile(m, tm, 16)
    tn = _tile(n_cols, tn, LANE)
    assert col_off % tn == 0
    joff = col_off // tn
    return pl.pallas_call(
        _matmul_kernel,
        grid=(m // tm, n_cols // tn),
        in_specs=[pl.BlockSpec((tm, k), lambda i, j: (i, 0)),
                  pl.BlockSpec((None, k, tn), lambda i, j: (layer, 0, j + joff))],
        out_specs=pl.BlockSpec((tm, tn), lambda i, j: (i, j)),
        out_shape=jax.ShapeDtypeStruct((m, n_cols), out_dtype),
        compiler_params=_params("parallel", "arbitrary"),
        name="matmul",
    )(a, w)


def _mm_res_norm_kernel(a_ref, w_ref, x_ref, g_ref, xo_ref, ho_ref, acc_ref, *, nk):
    k = pl.program_id(1)

    @pl.when(k == 0)
    def _():
        acc_ref[...] = x_ref[...]

    acc_ref[...] += _dot(a_ref[...], w_ref[...])

    @pl.when(k == nk - 1)
    def _():
        x = acc_ref[...]
        xo_ref[...] = x
        ho_ref[...] = (_rms(x) * g_ref[...]).astype(ho_ref.dtype)


def _mm_res_norm(a, w, layer, x, g, norm_dtype, *, tm=512, tk=1024):
    m, k = a.shape
    d = w.shape[2]
    tm = _tile(m, tm, 16)
    tk = _tile(k, tk, LANE)
    nk = k // tk
    return pl.pallas_call(
        functools.partial(_mm_res_norm_kernel, nk=nk),
        grid=(m // tm, nk),
        in_specs=[pl.BlockSpec((tm, tk), lambda i, kk: (i, kk)),
                  pl.BlockSpec((None, tk, d), lambda i, kk: (layer, kk, 0)),
                  pl.BlockSpec((tm, d), lambda i, kk: (i, 0)),
                  pl.BlockSpec((1, d), lambda i, kk: (0, 0))],
        out_specs=[pl.BlockSpec((tm, d), lambda i, kk: (i, 0)),
                   pl.BlockSpec((tm, d), lambda i, kk: (i, 0))],
        out_shape=[jax.ShapeDtypeStruct((m, d), F32), jax.ShapeDtypeStruct((m, d), norm_dtype)],
        scratch_shapes=[pltpu.VMEM((tm, d), F32)],
        compiler_params=_params("parallel", "arbitrary"),
        name="mm_res_norm",
    )(a, w, x, g.reshape(1, d))


def _ffn_in_kernel(h_ref, wg_ref, wu_ref, o_ref):
    h = h_ref[...]
    g = _dot(h, wg_ref[...])
    u = _dot(h, wu_ref[...])
    o_ref[...] = (_silu(g) * u).astype(o_ref.dtype)


def _ffn_in(h, w, layer, *, tm=1024, tn=512):
    m, d = h.shape
    dff = w.shape[2] // 2
    tm = _tile(m, tm, 16)
    tn = _tile(dff, tn, LANE)
    nj = dff // tn
    return pl.pallas_call(
        _ffn_in_kernel,
        grid=(m // tm, nj),
        in_specs=[pl.BlockSpec((tm, d), lambda i, j: (i, 0)),
                  pl.BlockSpec((None, d, tn), lambda i, j: (layer, 0, j)),
                  pl.BlockSpec((None, d, tn), lambda i, j: (layer, 0, j + nj))],
        out_specs=pl.BlockSpec((tm, tn), lambda i, j: (i, j)),
        out_shape=jax.ShapeDtypeStruct((m, dff), BF16),
        compiler_params=_params("parallel", "arbitrary"),
        name="ffn_in",
    )(h, w, w)


def _mix_kernel(ya_ref, yb_ref, ym_ref, wa_ref, wb_ref, wm_ref, ga_ref, gb_ref, gm_ref, o_ref):
    acc = jax.nn.sigmoid(ga_ref[...]) * _dot(ya_ref[...], wa_ref[...])
    acc = acc + jax.nn.sigmoid(gb_ref[...]) * _dot(yb_ref[...], wb_ref[...])
    acc = acc + jax.nn.sigmoid(gm_ref[...]) * _dot(ym_ref[...], wm_ref[...])
    o_ref[...] = acc.astype(o_ref.dtype)


def _mix(ya, yb, ym, wa, wb, wm, layer, p_f, gate_off, d, *, tm=512, tn=512):
    m = ya.shape[0]
    tm = _tile(m, tm, 16)
    tn = _tile(d, tn, LANE)
    assert gate_off % tn == 0
    goff = gate_off // tn
    nj = d // tn

    def y_spec(y):
        return pl.BlockSpec((tm, y.shape[1]), lambda i, j: (i, 0))

    def w_spec(w):
        return pl.BlockSpec((None, w.shape[1], tn), lambda i, j: (layer, 0, j))

    def g_spec(which):
        return pl.BlockSpec((tm, tn), lambda i, j: (i, goff + which * nj + j))

    return pl.pallas_call(
        _mix_kernel,
        grid=(m // tm, nj),
        in_specs=[y_spec(ya), y_spec(yb), y_spec(ym), w_spec(wa), w_spec(wb), w_spec(wm),
                  g_spec(0), g_spec(1), g_spec(2)],
        out_specs=pl.BlockSpec((tm, tn), lambda i, j: (i, j)),
        out_shape=jax.ShapeDtypeStruct((m, d), BF16),
        compiler_params=_params("parallel", "arbitrary"),
        name="mix",
    )(ya, yb, ym, wa, wb, wm, p_f, p_f, p_f)


def _memattn_kernel(q_ref, k_ref, v_ref, o_ref, *, heads, hd):
    for h in range(heads):
        sl = slice(h * hd, (h + 1) * hd)
        k = k_ref[:, sl].astype(BF16)
        v = v_ref[:, sl].astype(BF16)
        s = lax.dot_general(q_ref[:, sl], k, _NT, preferred_element_type=F32) * (hd ** -0.5)
        p = jnp.exp(s - jnp.max(s, axis=-1, keepdims=True))
        p = p / jnp.sum(p, axis=-1, keepdims=True)
        o_ref[:, sl] = _dot(p.astype(BF16), v).astype(o_ref.dtype)


def _memattn(p_bf, q_off, memk, memv, layer, batch, t, *, tq=512):
    n_mem, w = memk.shape[2], memk.shape[3]
    tq = _tile(t, tq, 16)
    nt = t // tq
    assert q_off % w == 0
    qoff = q_off // w
    return pl.pallas_call(
        functools.partial(_memattn_kernel, heads=M_HEADS, hd=w // M_HEADS),
        grid=(batch, nt),
        in_specs=[pl.BlockSpec((tq, w), lambda b, i: (b * nt + i, qoff)),
                  pl.BlockSpec((None, None, n_mem, w), lambda b, i: (layer, b, 0, 0)),
                  pl.BlockSpec((None, None, n_mem, w), lambda b, i: (layer, b, 0, 0))],
        out_specs=pl.BlockSpec((tq, w), lambda b, i: (b * nt + i, 0)),
        out_shape=jax.ShapeDtypeStruct((batch * t, w), BF16),
        compiler_params=_params("parallel", "arbitrary"),
        name="memattn",
    )(p_bf, memk, memv)


def _split3(x):
    hi = x.astype(BF16)
    r = x - hi.astype(F32)
    mid = r.astype(BF16)
    lo = (r - mid.astype(F32)).astype(BF16)
    return hi, mid, lo


def _mlstm_kernel(*refs, L, nc, heads, dqk, dv, has_state):
    if has_state:
        (qk_ref, v_ref, og_ref, ifc_ref, ifr_ref, bc_ref, br_ref, cw_ref, cb_ref, g_ref,
         c0_ref, n0_ref, m0_ref, conv0_ref,
         y_ref, c_out, n_out, m_out, conv_out, ubuf, caug, m_scr, vaug) = refs
    else:
        (qk_ref, v_ref, og_ref, ifc_ref, ifr_ref, bc_ref, br_ref, cw_ref, cb_ref, g_ref,
         y_ref, c_out, n_out, m_out, conv_out, ubuf, caug, m_scr, vaug) = refs
    c_idx = pl.program_id(1)
    nq = heads * dqk

    @pl.when(c_idx == 0)
    def _init():
        if has_state:
            ubuf[0:SUBLANE, :] = conv0_ref[...]
            caug[:, :, :dv] = c0_ref[...]
            caug[:, :, dv:] = n0_ref[...]
            m_scr[...] = m0_ref[...]
        else:
            ubuf[0:SUBLANE, :] = jnp.zeros((SUBLANE, 2 * nq), F32)
            caug[...] = jnp.zeros(caug.shape, F32)
            m_scr[...] = jnp.zeros(m_scr.shape, F32)
        lane = lax.broadcasted_iota(jnp.int32, (heads, L, LANE), 2)
        vaug[:, :, dv:] = jnp.where(lane == 0, 1.0, 0.0).astype(BF16)

    pre_c = ifc_ref[...] + bc_ref[...]
    pre_r = ifr_ref[...] + br_ref[...]
    lf_c = _log_sigmoid(pre_c)
    lf_r = _log_sigmoid(pre_r)
    row = lax.broadcasted_iota(jnp.int32, (L, L), 0)
    col = lax.broadcasted_iota(jnp.int32, (L, L), 1)
    causal = row >= col
    tri = jnp.where(causal, 1.0, 0.0).astype(BF16)
    tri_t = jnp.where(row <= col, 1.0, 0.0).astype(BF16)
    fcum_c = sum(_dot(tri, part) for part in _split3(lf_c))
    fcum_r = sum(_dot(part, tri_t) for part in _split3(lf_r))

    ubuf[SUBLANE:SUBLANE + L, :] = qk_ref[...]

    for h in range(heads):
        fc_c = fcum_c[:, heads + h:heads + h + 1]
        fc_r = fcum_r[heads + h:heads + h + 1, :]
        li_c = pre_c[:, h:h + 1]
        li_r = pre_r[h:h + 1, :]
        m_prev = m_scr[h, 0:1, 0:1]

        d = jnp.where(causal, fc_c - fc_r + li_r, NEG_BIG)
        inter = fc_c + m_prev
        m_t = jnp.maximum(inter, jnp.max(d, axis=1, keepdims=True))
        w_intra = jnp.exp(d - m_t)
        w_state = jnp.exp(inter - m_t)

        def conv(c0):
            cs = slice(c0, c0 + dqk)
            acc = cb_ref[:, cs] + cw_ref[CONV_W - 1:CONV_W, cs] * ubuf[SUBLANE:SUBLANE + L, cs]
            for j in range(CONV_W - 1):
                off = SUBLANE - (CONV_W - 1) + j
                acc = acc + cw_ref[j:j + 1, cs] * ubuf[off:off + L, cs]
            return _silu(acc)

        q = conv(h * dqk).astype(BF16)
        k = conv(nq + h * dqk) * (dqk ** -0.5)

        vaug[h, :, :dv] = v_ref[:, h * dv:(h + 1) * dv]
        va = vaug[h]

        s = lax.dot_general(q, k.astype(BF16), _NT, preferred_element_type=F32) * w_intra
        nd = _dot(s.astype(BF16), va) + w_state * _dot(q, caug[h].astype(BF16))
        num = nd[:, :dv]
        den = nd[:, dv:dv + 1]
        den = jnp.maximum(jnp.abs(den), jnp.exp(-m_t))
        hh = num / den
        vs = slice(h * dv, (h + 1) * dv)
        y = _rms(hh) * g_ref[:, vs] * jax.nn.sigmoid(og_ref[:, vs])
        y_ref[:, vs] = y.astype(y_ref.dtype)

        m_new = m_t[L - 1:L, :]
        fc_last = fc_c[L - 1:L, :]
        decay = jnp.exp(fc_last + m_prev - m_new)
        w_end = jnp.exp(fc_last - fc_c + li_c - m_new)
        upd = lax.dot_general((k * w_end).astype(BF16), va, _TN, preferred_element_type=F32)
        caug[h] = decay * caug[h] + upd
        m_scr[h] = jnp.broadcast_to(m_new, (SUBLANE, LANE))

    ubuf[0:SUBLANE, :] = ubuf[L:L + SUBLANE, :]

    @pl.when(c_idx == nc - 1)
    def _fin():
        c_out[...] = caug[:, :, :dv]
        n_out[...] = caug[:, :, dv:]
        m_out[...] = m_scr[...]
        conv_out[...] = ubuf[0:SUBLANE, :]


def _mlstm(p_f, qk_off, og_off, p_bf, v_off, p_if, b_i, b_f, conv_w, conv_b, g, state, layer, batch, t):
    heads, dqk, dv = A_HEADS, A_DQK, A_DV
    nq, nv = heads * dqk, heads * dv
    m_rows = batch * t
    L = min(MLSTM_CHUNK, t)
    nc = t // L
    assert t % L == 0 and L >= SUBLANE and qk_off % (2 * nq) == 0 and og_off % nv == 0 and v_off % nv == 0
    has_state = state is not None

    ifr = p_if[:, :SUBLANE].reshape(m_rows // L, L, SUBLANE).transpose(0, 2, 1)
    bias = jnp.concatenate([b_i, b_f]).astype(F32)
    bias_c = jnp.zeros((1, LANE), F32).at[0, :2 * heads].set(bias)
    bias_r = jnp.zeros((SUBLANE, 1), F32).at[:2 * heads, 0].set(bias)

    def rows(width, off):
        return pl.BlockSpec((L, width), lambda b, c: (b * nc + c, off // width))

    def whole(shape):
        return pl.BlockSpec(shape, lambda b, c: (0,) * len(shape))

    in_specs = [rows(2 * nq, qk_off), rows(nv, v_off), rows(nv, og_off), rows(LANE, 0),
                pl.BlockSpec((None, SUBLANE, L), lambda b, c: (b * nc + c, 0, 0)),
                whole((1, LANE)), whole((SUBLANE, 1)), whole((CONV_W, 2 * nq)), whole((1, 2 * nq)),
                whole((1, nv))]
    args = [p_f, p_bf, p_f, p_if, ifr, bias_c, bias_r, conv_w, conv_b.reshape(1, 2 * nq), g.reshape(1, nv)]
    if has_state:
        c0, n0, m0, conv0 = state
        n0p = jnp.pad(n0[layer][..., None], ((0, 0), (0, 0), (0, 0), (0, LANE - 1)))
        m0p = jnp.broadcast_to(m0[layer][:, :, None, None], (batch, heads, SUBLANE, LANE))
        conv0p = jnp.pad(conv0[layer], ((0, 0), (SUBLANE - (CONV_W - 1), 0), (0, 0)))
        in_specs += [pl.BlockSpec((None, None, heads, dqk, dv), lambda b, c: (layer, b, 0, 0, 0)),
                     pl.BlockSpec((None, heads, dqk, LANE), lambda b, c: (b, 0, 0, 0)),
                     pl.BlockSpec((None, heads, SUBLANE, LANE), lambda b, c: (b, 0, 0, 0)),
                     pl.BlockSpec((None, SUBLANE, 2 * nq), lambda b, c: (b, 0, 0))]
        args += [c0, n0p, m0p, conv0p]

    out_specs = [pl.BlockSpec((L, nv), lambda b, c: (b * nc + c, 0)),
                 pl.BlockSpec((None, heads, dqk, dv), lambda b, c: (b, 0, 0, 0)),
                 pl.BlockSpec((None, heads, dqk, LANE), lambda b, c: (b, 0, 0, 0)),
                 pl.BlockSpec((None, heads, SUBLANE, LANE), lambda b, c: (b, 0, 0, 0)),
                 pl.BlockSpec((None, SUBLANE, 2 * nq), lambda b, c: (b, 0, 0))]
    out_shape = [jax.ShapeDtypeStruct((m_rows, nv), BF16),
                 jax.ShapeDtypeStruct((batch, heads, dqk, dv), F32),
                 jax.ShapeDtypeStruct((batch, heads, dqk, LANE), F32),
                 jax.ShapeDtypeStruct((batch, heads, SUBLANE, LANE), F32),
                 jax.ShapeDtypeStruct((batch, SUBLANE, 2 * nq), F32)]
    ya, c_new, n_new, m_new, conv_new = pl.pallas_call(
        functools.partial(_mlstm_kernel, L=L, nc=nc, heads=heads, dqk=dqk, dv=dv, has_state=has_state),
        grid=(batch, nc),
        in_specs=in_specs,
        out_specs=out_specs,
        out_shape=out_shape,
        scratch_shapes=[pltpu.VMEM((SUBLANE + L, 2 * nq), F32),
                        pltpu.VMEM((heads, dqk, dv + LANE), F32),
                        pltpu.VMEM((heads, SUBLANE, LANE), F32),
                        pltpu.VMEM((heads, L, dv + LANE), BF16)],
        compiler_params=_params("parallel", "arbitrary"),
        name="mlstm",
    )(*args)
    return ya, c_new, n_new[..., 0], m_new[:, :, 0, 0], conv_new[:, SUBLANE - (CONV_W - 1):, :]


def _cumsum_rows(x, L):
    row = lax.broadcasted_iota(jnp.int32, x.shape, 0)
    sh = 1
    while sh < L:
        x = x + jnp.where(row >= sh, pltpu.roll(x, sh, 0), 0.0)
        sh *= 2
    return x


def _ref_rows(g_scr, L, bs, n):
    hs = bs // 2

    def bcast(r, rows):
        return jnp.broadcast_to(g_scr[r:r + 1, :], (rows, n))

    if bs >= 2 * SUBLANE:
        return jnp.concatenate([bcast(b0 + hs - 1, bs) for b0 in range(0, L, bs)], axis=0)
    sub = lax.broadcasted_iota(jnp.int32, (SUBLANE, n), 0)
    groups = []
    for g0 in range(0, L, SUBLANE):
        r0 = g0 + hs - 1
        val = bcast(r0, SUBLANE)
        for b0 in range(bs, SUBLANE, bs):
            val = jnp.where(sub >= b0, bcast(r0 + b0, SUBLANE), val)
        groups.append(val)
    return jnp.concatenate(groups, axis=0)


def _hgrn_kernel(*refs, L, nc, heads, dk, dv, layer, has_state):
    if has_state:
        q_ref, f_ref, v_ref, gt_ref, lbraw_ref, g_ref, s0_ref, y_ref, s_out, st, lb_scr, g_scr = refs
    else:
        q_ref, f_ref, v_ref, gt_ref, lbraw_ref, g_ref, y_ref, s_out, st, lb_scr, g_scr = refs
    c_idx = pl.program_id(1)

    @pl.when(c_idx == 0)
    def _init():
        for h in range(heads):
            st[h] = s0_ref[h].T if has_state else jnp.zeros((dv, dk), F32)
        raw = lbraw_ref[...]
        e = jnp.exp(raw - jnp.max(raw, axis=0, keepdims=True))
        sm = e / jnp.sum(e, axis=0, keepdims=True)
        lb = jnp.zeros((1, heads * dk), F32)
        for j in range(1, layer + 1):
            lb = lb + sm[j:j + 1, :]
        lb_scr[...] = lb

    row = lax.broadcasted_iota(jnp.int32, (L, L), 0)
    col = lax.broadcasted_iota(jnp.int32, (L, L), 1)
    trow = lax.broadcasted_iota(jnp.int32, (L, dk), 0)
    levels = []
    bs = 2
    while bs <= L:
        hs, lg = bs // 2, bs.bit_length() - 1
        pair = ((row >> lg) == (col >> lg)) & ((row & (bs - 1)) >= hs) & ((col & (bs - 1)) < hs)
        levels.append((bs, pair, (trow & (bs - 1)) >= hs))
        bs *= 2

    def head(h, carry):
        kc = pl.ds(pl.multiple_of(h * dk, LANE), dk)
        vc = pl.ds(pl.multiple_of(h * dv, LANE), dv)
        z = f_ref[:, kc]
        lb = lb_scr[:, kc]
        log_f = _log_sigmoid(z) + jnp.log1p(lb * jnp.exp(-z))
        kb = (1.0 - lb) * jax.nn.sigmoid(-z)
        qb = _silu(q_ref[:, kc])
        g = _cumsum_rows(log_f, L)
        g_scr[...] = g
        v = v_ref[:, vc]

        a = jnp.where(row == col, lax.dot_general(qb.astype(BF16), kb.astype(BF16), _NT,
                                                  preferred_element_type=F32), 0.0)
        for bs, pair, upper in levels:
            e = jnp.exp(-jnp.abs(g - _ref_rows(g_scr, L, bs, dk)))
            qt = jnp.where(upper, qb * e, 0.0).astype(BF16)
            kt = jnp.where(upper, 0.0, kb * e).astype(BF16)
            a = a + jnp.where(pair, lax.dot_general(qt, kt, _NT, preferred_element_type=F32), 0.0)

        s_t = st[h]
        o = _dot(a.astype(BF16), v) + lax.dot_general((qb * jnp.exp(g)).astype(BF16), s_t.astype(BF16), _NT,
                                                      preferred_element_type=F32)
        y = _rms(o) * g_ref[...] * _silu(gt_ref[:, vc])
        y_ref[:, vc] = y.astype(y_ref.dtype)

        g_end = g[L - 1:L, :]
        ke = (kb * jnp.exp(g_end - g)).astype(BF16)
        st[h] = jnp.exp(g_end) * s_t + lax.dot_general(v, ke, _TN, preferred_element_type=F32)
        return carry

    lax.fori_loop(0, heads, head, 0)

    @pl.when(c_idx == nc - 1)
    def _fin():
        for h in range(heads):
            s_out[h] = st[h].T


def _hgrn(p_f, q_off, f_off, gt_off, p_bf, v_off, lb_raw, g, s0, layer, batch, t):
    heads, dk, dv = B_HEADS, B_DK, B_DV
    nk, nv = heads * dk, heads * dv
    L = min(HGRN_CHUNK, t)
    nc = t // L
    assert t % L == 0 and q_off % nk == 0 and f_off % nk == 0 and gt_off % nv == 0 and v_off % nv == 0
    has_state = s0 is not None

    def rows(width, off):
        return pl.BlockSpec((L, width), lambda b, c: (b * nc + c, off // width))

    in_specs = [rows(nk, q_off), rows(nk, f_off), rows(nv, v_off), rows(nv, gt_off),
                pl.BlockSpec(lb_raw.shape, lambda b, c: (0, 0)),
                pl.BlockSpec((1, dv), lambda b, c: (0, 0))]
    args = [p_f, p_f, p_bf, p_f, lb_raw, g.reshape(1, dv)]
    if has_state:
        in_specs.append(pl.BlockSpec((None, None, heads, dk, dv), lambda b, c: (layer, b, 0, 0, 0)))
        args.append(s0)
    return pl.pallas_call(
        functools.partial(_hgrn_kernel, L=L, nc=nc, heads=heads, dk=dk, dv=dv, layer=layer,
                          has_state=has_state),
        grid=(batch, nc),
        in_specs=in_specs,
        out_specs=[pl.BlockSpec((L, nv), lambda b, c: (b * nc + c, 0)),
                   pl.BlockSpec((None, heads, dk, dv), lambda b, c: (b, 0, 0, 0))],
        out_shape=[jax.ShapeDtypeStruct((batch * t, nv), BF16),
                   jax.ShapeDtypeStruct((batch, heads, dk, dv), F32)],
        scratch_shapes=[pltpu.VMEM((heads, dv, dk), F32),
                        pltpu.VMEM((1, nk), F32),
                        pltpu.VMEM((L, dk), F32)],
        compiler_params=_params("parallel", "arbitrary"),
        name="hgrn",
    )(*args)


def _prep_weights(w_in, w_proj_a, w_proj_b, w_proj_m, w_out, w_ffn_in, w_ffn_out, w_mem_kv):
    a_qk = 2 * A_HEADS * A_DQK
    a_v = A_HEADS * A_DV
    b_k = B_HEADS * B_DK
    b_v = B_HEADS * B_DV
    m_w = M_HEADS * M_HD
    sizes = [a_qk, a_v, a_v, 2 * A_HEADS, b_k, b_k, b_v, b_v, m_w]
    starts = [0]
    for s in sizes:
        starts.append(starts[-1] + s)
    qk_a, v_a, o_a, if_a, q_b, f_b, i_b, gt_b, q_m = [
        w_in[:, :, starts[i]:starts[i + 1]] for i in range(len(sizes))]
    gates = w_in[:, :, starts[-1]:]
    w_bf = jnp.concatenate([v_a, i_b, q_m], axis=-1).astype(BF16)
    w_f = jnp.concatenate([qk_a, o_a, q_b, f_b, gt_b, gates], axis=-1).astype(BF16)
    w_if = jnp.pad(if_a, ((0, 0), (0, 0), (0, LANE - 2 * A_HEADS))).astype(BF16)
    offs = dict(v_a=0, i_b=a_v, q_m=a_v + b_v,
                qk_a=0, o_a=a_qk, q_b=a_qk + a_v, f_b=a_qk + a_v + b_k, gt_b=a_qk + a_v + 2 * b_k,
                gates=a_qk + a_v + 2 * b_k + b_v)
    return dict(w_bf=w_bf, w_f=w_f, w_if=w_if, offs=offs,
                w_proj_a=w_proj_a.astype(BF16), w_proj_b=w_proj_b.astype(BF16),
                w_proj_m=w_proj_m.astype(BF16), w_out=w_out.astype(BF16),
                w_ffn_in=w_ffn_in.astype(BF16), w_ffn_out=w_ffn_out.astype(BF16),
                w_mem_kv=w_mem_kv.astype(BF16))


def _trunk(x, memk, memv, state, W, P):
    batch, t, d = x.shape
    offs = W["offs"]
    x2 = x.reshape(batch * t, d)
    h = _rmsnorm(x2, P["g_mix"][0], BF16)
    new_conv, new_c, new_n, new_m, new_s = [], [], [], [], []
    for l in range(DEPTH):
        p_bf = _matmul(h, W["w_bf"], l, BF16)
        p_f = _matmul(h, W["w_f"], l, F32)
        p_if = _matmul(h, W["w_if"], l, F32)

        ya, ca, na, ma, cbuf = _mlstm(
            p_f, offs["qk_a"], offs["o_a"], p_bf, offs["v_a"], p_if, P["b_igate"][l], P["b_fgate"][l],
            P["conv_w"][l], P["conv_b"][l], P["g_mlstm"][l],
            None if state is None else state[:4], l, batch, t)
        yb, sb = _hgrn(p_f, offs["q_b"], offs["f_b"], offs["gt_b"], p_bf, offs["i_b"], P["lb_raw"],
                       P["g_hgrn"][l], None if state is None else state[4], l, batch, t)
        ym = _memattn(p_bf, offs["q_m"], memk, memv, l, batch, t)

        mix = _mix(ya, yb, ym, W["w_proj_a"], W["w_proj_b"], W["w_proj_m"], l, p_f, offs["gates"], d)
        x2, h2 = _mm_res_norm(mix, W["w_out"], l, x2, P["g_ffn"][l], BF16)
        act = _ffn_in(h2, W["w_ffn_in"], l)
        last = l == DEPTH - 1
        x2, h = _mm_res_norm(act, W["w_ffn_out"], l, x2, P["g_final"] if last else P["g_mix"][l + 1],
                             F32 if last else BF16)
        new_conv.append(cbuf)
        new_c.append(ca)
        new_n.append(na)
        new_m.append(ma)
        new_s.append(sb)
    return (h.reshape(batch, t, d), jnp.stack(new_conv), jnp.stack(new_c), jnp.stack(new_n),
            jnp.stack(new_m), jnp.stack(new_s))


def kernel(x_prompt, x_sample, mem_prompt, cache_mem_k, cache_mem_v, state_mlstm_c, state_mlstm_n, state_mlstm_m, state_mlstm_conv, state_hgrn_s, g_mix, w_in, conv_w, conv_b, b_igate, b_fgate, g_mlstm, lb_raw, g_hgrn, g_mem, w_mem_kv, w_proj_a, w_proj_b, w_proj_m, w_out, g_ffn, w_ffn_in, w_ffn_out, g_final):
    W = _prep_weights(w_in, w_proj_a, w_proj_b, w_proj_m, w_out, w_ffn_in, w_ffn_out, w_mem_kv)
    P = dict(g_mix=g_mix, conv_w=conv_w, conv_b=conv_b, b_igate=b_igate, b_fgate=b_fgate, g_mlstm=g_mlstm,
             lb_raw=lb_raw, g_hgrn=g_hgrn, g_ffn=g_ffn, g_final=g_final)
    batch, n_mem, d = mem_prompt.shape
    dec_batch = x_sample.shape[0]
    m_w = M_HEADS * M_HD

    mem2 = mem_prompt.reshape(batch * n_mem, d)
    ks, vs = [], []
    for l in range(DEPTH):
        mem_n = _rmsnorm(mem2, g_mem[l], BF16)
        ks.append(_matmul(mem_n, W["w_mem_kv"], l, F32, col_off=0, n_cols=m_w))
        vs.append(_matmul(mem_n, W["w_mem_kv"], l, F32, col_off=m_w, n_cols=m_w))
    mem_k = jnp.stack(ks).reshape(DEPTH, batch, n_mem, m_w)
    mem_v = jnp.stack(vs).reshape(DEPTH, batch, n_mem, m_w)

    y_p, conv_p, c_p, n_p, m_p, s_p = _trunk(x_prompt, mem_k, mem_v, None, W, P)

    state = (state_mlstm_c, state_mlstm_n, state_mlstm_m, state_mlstm_conv, state_hgrn_s)
    y_s, conv_s, c_s, n_s, m_s, s_s = _trunk(
        x_sample, cache_mem_k.reshape(DEPTH, dec_batch, n_mem, m_w),
        cache_mem_v.reshape(DEPTH, dec_batch, n_mem, m_w), state, W, P)

    mem_shape = (DEPTH, batch, n_mem, M_HEADS, M_HD)
    return (y_p, y_s, mem_k.reshape(mem_shape), mem_v.reshape(mem_shape), conv_p, c_p, n_p, m_p, s_p,
            conv_s, c_s, n_s, m_s, s_s)
```

```python
import functools

import jax
import jax.numpy as jnp
from jax import lax
from jax.experimental import pallas as pl
from jax.experimental.pallas import tpu as pltpu

F32 = jnp.float32
BF16 = jnp.bfloat16

EPS = 1e-6
NEG_BIG = -1e30
DEPTH = 4
CONV_W = 4
A_HEADS = 4
A_DQK = 256
A_DV = 512
B_HEADS = 16
B_DK = 128
B_DV = 128
M_HEADS = 4
M_HD = 512

LANE = 128
SUBLANE = 8
VMEM_LIMIT = 56 * 1024 * 1024
MLSTM_CHUNK = 128
HGRN_CHUNK = 64
HGRN_UNROLL = 4
LOG2E = 1.4426950408889634

_NT = (((1,), (1,)), ((), ()))
_TN = (((0,), (0,)), ((), ()))


def _params(*sem):
    return pltpu.CompilerParams(dimension_semantics=sem, vmem_limit_bytes=VMEM_LIMIT)


def _tile(n, pref, align):
    best = None
    t = align
    while t <= min(n, pref):
        if n % t == 0:
            best = t
        t += align
    return best if best is not None else n


def _dot(a, b):
    return jnp.dot(a, b, preferred_element_type=F32)


def _log_sigmoid(x):
    return jnp.minimum(x, 0.0) - jnp.log1p(jnp.exp(-jnp.abs(x)))


def _silu(x):
    return x * jax.nn.sigmoid(x)


def _silu_tanh(x):
    hx = 0.5 * x
    return hx + hx * jnp.tanh(hx)


def _rms(x):
    return x * lax.rsqrt(jnp.mean(x * x, axis=-1, keepdims=True) + EPS)


def _rmsnorm_kernel(x_ref, g_ref, o_ref):
    o_ref[...] = (_rms(x_ref[...]) * g_ref[...]).astype(o_ref.dtype)


def _rmsnorm(x, g, out_dtype):
    m, d = x.shape
    tm = _tile(m, 512, 16)
    return pl.pallas_call(
        _rmsnorm_kernel,
        grid=(m // tm,),
        in_specs=[pl.BlockSpec((tm, d), lambda i: (i, 0)), pl.BlockSpec((1, d), lambda i: (0, 0))],
        out_specs=pl.BlockSpec((tm, d), lambda i: (i, 0)),
        out_shape=jax.ShapeDtypeStruct((m, d), out_dtype),
        compiler_params=_params("parallel"),
        name="rmsnorm",
    )(x, g.reshape(1, d))


def _matmul_kernel(a_ref, w_ref, o_ref):
    o_ref[...] = _dot(a_ref[...], w_ref[...]).astype(o_ref.dtype)


def _matmul(a, w, layer, out_dtype, *, col_off=0, n_cols=None, tm=2048, tn=512):
    m, k = a.shape
    n_cols = w.shape[2] - col_off if n_cols is None else n_cols
    tm = _tile(m, tm, 16)
    tn = _tile(n_cols, tn, LANE)
    assert col_off % tn == 0
    joff = col_off // tn
    return pl.pallas_call(
        _matmul_kernel,
        grid=(m // tm, n_cols // tn),
        in_specs=[pl.BlockSpec((tm, k), lambda i, j: (i, 0)),
                  pl.BlockSpec((None, k, tn), lambda i, j: (layer, 0, j + joff))],
        out_specs=pl.BlockSpec((tm, tn), lambda i, j: (i, j)),
        out_shape=jax.ShapeDtypeStruct((m, n_cols), out_dtype),
        compiler_params=_params("parallel", "arbitrary"),
        name="matmul",
    )(a, w)


def _mm_res_norm_kernel(a_ref, w_ref, x_ref, g_ref, xo_ref, ho_ref):
    x = x_ref[...] + _dot(a_ref[...], w_ref[...])
    xo_ref[...] = x
    ho_ref[...] = (_rms(x) * g_ref[...]).astype(ho_ref.dtype)


def _mm_res_norm(a, w, layer, x, g, norm_dtype, *, tm):
    m, k = a.shape
    d = w.shape[2]
    tm = _tile(m, tm, 16)
    return pl.pallas_call(
        _mm_res_norm_kernel,
        grid=(m // tm,),
        in_specs=[pl.BlockSpec((tm, k), lambda i: (i, 0)),
                  pl.BlockSpec((None, k, d), lambda i: (layer, 0, 0), pipeline_mode=pl.Buffered(1)),
                  pl.BlockSpec((tm, d), lambda i: (i, 0)),
                  pl.BlockSpec((1, d), lambda i: (0, 0))],
        out_specs=[pl.BlockSpec((tm, d), lambda i: (i, 0)),
                   pl.BlockSpec((tm, d), lambda i: (i, 0))],
        out_shape=[jax.ShapeDtypeStruct((m, d), F32), jax.ShapeDtypeStruct((m, d), norm_dtype)],
        compiler_params=_params("parallel"),
        name="mm_res_norm",
    )(a, w, x, g.reshape(1, d))


def _ffn_in_kernel(h_ref, wg_ref, wu_ref, o_ref):
    h = h_ref[...]
    g = _dot(h, wg_ref[...])
    u = _dot(h, wu_ref[...])
    o_ref[...] = (_silu(g) * u).astype(o_ref.dtype)


def _ffn_in(h, w, layer, *, tm=2048, tn=512):
    m, d = h.shape
    dff = w.shape[2] // 2
    tm = _tile(m, tm, 16)
    tn = _tile(dff, tn, LANE)
    nj = dff // tn
    return pl.pallas_call(
        _ffn_in_kernel,
        grid=(m // tm, nj),
        in_specs=[pl.BlockSpec((tm, d), lambda i, j: (i, 0)),
                  pl.BlockSpec((None, d, tn), lambda i, j: (layer, 0, j)),
                  pl.BlockSpec((None, d, tn), lambda i, j: (layer, 0, j + nj))],
        out_specs=pl.BlockSpec((tm, tn), lambda i, j: (i, j)),
        out_shape=jax.ShapeDtypeStruct((m, dff), BF16),
        compiler_params=_params("parallel", "arbitrary"),
        name="ffn_in",
    )(h, w, w)


def _mix_kernel(ya_ref, yb_ref, ym_ref, wa_ref, wb_ref, wm_ref, ga_ref, gb_ref, gm_ref, o_ref):
    def gate(ref):
        return jax.nn.sigmoid(ref[...].astype(F32))

    acc = gate(ga_ref) * _dot(ya_ref[...], wa_ref[...])
    acc = acc + gate(gb_ref) * _dot(yb_ref[...], wb_ref[...])
    acc = acc + gate(gm_ref) * _dot(ym_ref[...], wm_ref[...])
    o_ref[...] = acc.astype(o_ref.dtype)


def _mix(ya, yb, ym, wa, wb, wm, layer, p_f, gate_off, d, *, tm=1024, tn=256):
    m = ya.shape[0]
    tm = _tile(m, tm, 16)
    tn = _tile(d, tn, LANE)
    assert gate_off % tn == 0
    goff = gate_off // tn
    nj = d // tn

    def y_spec(y):
        return pl.BlockSpec((tm, y.shape[1]), lambda i, j: (i, 0))

    def w_spec(w):
        return pl.BlockSpec((None, w.shape[1], tn), lambda i, j: (layer, 0, j))

    def g_spec(which):
        return pl.BlockSpec((tm, tn), lambda i, j: (i, goff + which * nj + j))

    return pl.pallas_call(
        _mix_kernel,
        grid=(m // tm, nj),
        in_specs=[y_spec(ya), y_spec(yb), y_spec(ym), w_spec(wa), w_spec(wb), w_spec(wm),
                  g_spec(0), g_spec(1), g_spec(2)],
        out_specs=pl.BlockSpec((tm, tn), lambda i, j: (i, j)),
        out_shape=jax.ShapeDtypeStruct((m, d), BF16),
        compiler_params=_params("parallel", "arbitrary"),
        name="mix",
    )(ya, yb, ym, wa, wb, wm, p_f, p_f, p_f)


def _memattn_kernel(q_ref, k_ref, v_ref, o_ref, *, heads, hd):
    for h in range(heads):
        sl = slice(h * hd, (h + 1) * hd)
        k = k_ref[:, sl].astype(BF16)
        v = v_ref[:, sl].astype(BF16)
        s = lax.dot_general(q_ref[:, sl], k, _NT, preferred_element_type=F32) * (hd ** -0.5)
        p = jnp.exp(s - jnp.max(s, axis=-1, keepdims=True))
        p = p / jnp.sum(p, axis=-1, keepdims=True)
        o_ref[:, sl] = _dot(p.astype(BF16), v).astype(o_ref.dtype)


def _memattn(p_bf, q_off, memk, memv, layer, batch, t, *, tq=512):
    n_mem, w = memk.shape[2], memk.shape[3]
    tq = _tile(t, tq, 16)
    nt = t // tq
    assert q_off % w == 0
    qoff = q_off // w
    return pl.pallas_call(
        functools.partial(_memattn_kernel, heads=M_HEADS, hd=w // M_HEADS),
        grid=(batch, nt),
        in_specs=[pl.BlockSpec((tq, w), lambda b, i: (b * nt + i, qoff)),
                  pl.BlockSpec((None, None, n_mem, w), lambda b, i: (layer, b, 0, 0)),
                  pl.BlockSpec((None, None, n_mem, w), lambda b, i: (layer, b, 0, 0))],
        out_specs=pl.BlockSpec((tq, w), lambda b, i: (b * nt + i, 0)),
        out_shape=jax.ShapeDtypeStruct((batch * t, w), BF16),
        compiler_params=_params("parallel", "arbitrary"),
        name="memattn",
    )(p_bf, memk, memv)


def _split3(x):
    hi = x.astype(BF16)
    r = x - hi.astype(F32)
    mid = r.astype(BF16)
    lo = (r - mid.astype(F32)).astype(BF16)
    return hi, mid, lo


def _mlstm_kernel(*refs, L, nc, heads, dqk, dv, has_state):
    if has_state:
        (qk_ref, v_ref, og_ref, ifc_ref, ifr_ref, bc_ref, br_ref, cw_ref, cb_ref, g_ref,
         c0_ref, n0_ref, m0_ref, conv0_ref,
         y_ref, c_out, n_out, m_out, conv_out, ubuf, caug, m_scr, vaug) = refs
    else:
        (qk_ref, v_ref, og_ref, ifc_ref, ifr_ref, bc_ref, br_ref, cw_ref, cb_ref, g_ref,
         y_ref, c_out, n_out, m_out, conv_out, ubuf, caug, m_scr, vaug) = refs
    c_idx = pl.program_id(1)
    nq = heads * dqk

    @pl.when(c_idx == 0)
    def _init():
        if has_state:
            ubuf[0:SUBLANE, :] = conv0_ref[...]
            caug[:, :, :dv] = c0_ref[...]
            caug[:, :, dv:] = n0_ref[...]
            m_scr[...] = m0_ref[...]
        else:
            ubuf[0:SUBLANE, :] = jnp.zeros((SUBLANE, 2 * nq), F32)
            caug[...] = jnp.zeros(caug.shape, F32)
            m_scr[...] = jnp.zeros(m_scr.shape, F32)
        lane = lax.broadcasted_iota(jnp.int32, (heads, L, LANE), 2)
        vaug[:, :, dv:] = jnp.where(lane == 0, 1.0, 0.0).astype(BF16)

    pre_c = ifc_ref[...] + bc_ref[...]
    pre_r = ifr_ref[...] + br_ref[...]
    lf_c = _log_sigmoid(pre_c)
    lf_r = _log_sigmoid(pre_r)
    row = lax.broadcasted_iota(jnp.int32, (L, L), 0)
    col = lax.broadcasted_iota(jnp.int32, (L, L), 1)
    causal = row >= col
    tri = jnp.where(causal, 1.0, 0.0).astype(BF16)
    tri_t = jnp.where(row <= col, 1.0, 0.0).astype(BF16)
    fcum_c = sum(_dot(tri, part) for part in _split3(lf_c))
    fcum_r = sum(_dot(part, tri_t) for part in _split3(lf_r))

    ubuf[SUBLANE:SUBLANE + L, :] = qk_ref[...].astype(F32)

    for h in range(heads):
        fc_c = fcum_c[:, heads + h:heads + h + 1]
        fc_r = fcum_r[heads + h:heads + h + 1, :]
        li_c = pre_c[:, h:h + 1]
        li_r = pre_r[h:h + 1, :]
        m_prev = m_scr[h, 0:1, 0:1]

        d = jnp.where(causal, fc_c - fc_r + li_r, NEG_BIG)
        inter = fc_c + m_prev
        m_t = jnp.maximum(inter, jnp.max(d, axis=1, keepdims=True))
        w_intra = jnp.exp(d - m_t)
        w_state = jnp.exp(inter - m_t)

        def conv(c0):
            cs = slice(c0, c0 + dqk)
            acc = cb_ref[:, cs] + cw_ref[CONV_W - 1:CONV_W, cs] * ubuf[SUBLANE:SUBLANE + L, cs]
            for j in range(CONV_W - 1):
                off = SUBLANE - (CONV_W - 1) + j
                acc = acc + cw_ref[j:j + 1, cs] * ubuf[off:off + L, cs]
            return _silu(acc)

        q = conv(h * dqk).astype(BF16)
        k = conv(nq + h * dqk) * (dqk ** -0.5)

        vaug[h, :, :dv] = v_ref[:, h * dv:(h + 1) * dv]
        va = vaug[h]

        s = lax.dot_general(q, k.astype(BF16), _NT, preferred_element_type=F32) * w_intra
        nd = _dot(s.astype(BF16), va) + w_state * _dot(q, caug[h].astype(BF16))
        num = nd[:, :dv]
        den = nd[:, dv:dv + 1]
        den = jnp.maximum(jnp.abs(den), jnp.exp(-m_t))
        hh = num / den
        vs = slice(h * dv, (h + 1) * dv)
        y = _rms(hh) * g_ref[:, vs] * jax.nn.sigmoid(og_ref[:, vs].astype(F32))
        y_ref[:, vs] = y.astype(y_ref.dtype)

        m_new = m_t[L - 1:L, :]
        fc_last = fc_c[L - 1:L, :]
        decay = jnp.exp(fc_last + m_prev - m_new)
        w_end = jnp.exp(fc_last - fc_c + li_c - m_new)
        upd = lax.dot_general((k * w_end).astype(BF16), va, _TN, preferred_element_type=F32)
        caug[h] = decay * caug[h] + upd
        m_scr[h] = jnp.broadcast_to(m_new, (SUBLANE, LANE))

    ubuf[0:SUBLANE, :] = ubuf[L:L + SUBLANE, :]

    @pl.when(c_idx == nc - 1)
    def _fin():
        c_out[...] = caug[:, :, :dv]
        n_out[...] = caug[:, :, dv:]
        m_out[...] = m_scr[...]
        conv_out[...] = ubuf[0:SUBLANE, :]


def _mlstm(p_f, qk_off, og_off, p_bf, v_off, p_if, b_i, b_f, conv_w, conv_b, g, state, layer, batch, t):
    heads, dqk, dv = A_HEADS, A_DQK, A_DV
    nq, nv = heads * dqk, heads * dv
    m_rows = batch * t
    L = min(MLSTM_CHUNK, t)
    nc = t // L
    assert t % L == 0 and L >= SUBLANE and qk_off % (2 * nq) == 0 and og_off % nv == 0 and v_off % nv == 0
    has_state = state is not None

    ifr = p_if[:, :SUBLANE].reshape(m_rows // L, L, SUBLANE).transpose(0, 2, 1)
    bias = jnp.concatenate([b_i, b_f]).astype(F32)
    bias_c = jnp.zeros((1, LANE), F32).at[0, :2 * heads].set(bias)
    bias_r = jnp.zeros((SUBLANE, 1), F32).at[:2 * heads, 0].set(bias)

    def rows(width, off):
        return pl.BlockSpec((L, width), lambda b, c: (b * nc + c, off // width))

    def whole(shape):
        return pl.BlockSpec(shape, lambda b, c: (0,) * len(shape))

    in_specs = [rows(2 * nq, qk_off), rows(nv, v_off), rows(nv, og_off), rows(LANE, 0),
                pl.BlockSpec((None, SUBLANE, L), lambda b, c: (b * nc + c, 0, 0)),
                whole((1, LANE)), whole((SUBLANE, 1)), whole((CONV_W, 2 * nq)), whole((1, 2 * nq)),
                whole((1, nv))]
    args = [p_f, p_bf, p_f, p_if, ifr, bias_c, bias_r, conv_w, conv_b.reshape(1, 2 * nq), g.reshape(1, nv)]
    if has_state:
        c0, n0, m0, conv0 = state
        n0p = jnp.pad(n0[layer][..., None], ((0, 0), (0, 0), (0, 0), (0, LANE - 1)))
        m0p = jnp.broadcast_to(m0[layer][:, :, None, None], (batch, heads, SUBLANE, LANE))
        conv0p = jnp.pad(conv0[layer], ((0, 0), (SUBLANE - (CONV_W - 1), 0), (0, 0)))
        in_specs += [pl.BlockSpec((None, None, heads, dqk, dv), lambda b, c: (layer, b, 0, 0, 0)),
                     pl.BlockSpec((None, heads, dqk, LANE), lambda b, c: (b, 0, 0, 0)),
                     pl.BlockSpec((None, heads, SUBLANE, LANE), lambda b, c: (b, 0, 0, 0)),
                     pl.BlockSpec((None, SUBLANE, 2 * nq), lambda b, c: (b, 0, 0))]
        args += [c0, n0p, m0p, conv0p]

    out_specs = [pl.BlockSpec((L, nv), lambda b, c: (b * nc + c, 0)),
                 pl.BlockSpec((None, heads, dqk, dv), lambda b, c: (b, 0, 0, 0)),
                 pl.BlockSpec((None, heads, dqk, LANE), lambda b, c: (b, 0, 0, 0)),
                 pl.BlockSpec((None, heads, SUBLANE, LANE), lambda b, c: (b, 0, 0, 0)),
                 pl.BlockSpec((None, SUBLANE, 2 * nq), lambda b, c: (b, 0, 0))]
    out_shape = [jax.ShapeDtypeStruct((m_rows, nv), BF16),
                 jax.ShapeDtypeStruct((batch, heads, dqk, dv), F32),
                 jax.ShapeDtypeStruct((batch, heads, dqk, LANE), F32),
                 jax.ShapeDtypeStruct((batch, heads, SUBLANE, LANE), F32),
                 jax.ShapeDtypeStruct((batch, SUBLANE, 2 * nq), F32)]
    ya, c_new, n_new, m_new, conv_new = pl.pallas_call(
        functools.partial(_mlstm_kernel, L=L, nc=nc, heads=heads, dqk=dqk, dv=dv, has_state=has_state),
        grid=(batch, nc),
        in_specs=in_specs,
        out_specs=out_specs,
        out_shape=out_shape,
        scratch_shapes=[pltpu.VMEM((SUBLANE + L, 2 * nq), F32),
                        pltpu.VMEM((heads, dqk, dv + LANE), F32),
                        pltpu.VMEM((heads, SUBLANE, LANE), F32),
                        pltpu.VMEM((heads, L, dv + LANE), BF16)],
        compiler_params=_params("parallel", "arbitrary"),
        name="mlstm",
    )(*args)
    return ya, c_new, n_new[..., 0], m_new[:, :, 0, 0], conv_new[:, SUBLANE - (CONV_W - 1):, :]


def _neg_abs(x):
    bits = lax.bitcast_convert_type(x, jnp.uint32) | jnp.uint32(0x80000000)
    return lax.bitcast_convert_type(bits, F32)


def _ref_rows(g_scr, L, bs, n):
    hs = bs // 2

    def bcast(r, rows):
        return jnp.broadcast_to(g_scr[r:r + 1, :], (rows, n))

    if bs >= 2 * SUBLANE:
        return jnp.concatenate([bcast(b0 + hs - 1, bs) for b0 in range(0, L, bs)], axis=0)
    sub = lax.broadcasted_iota(jnp.int32, (SUBLANE, n), 0)
    groups = []
    for g0 in range(0, L, SUBLANE):
        r0 = g0 + hs - 1
        val = bcast(r0, SUBLANE)
        for b0 in range(bs, SUBLANE, bs):
            val = jnp.where(sub >= b0, bcast(r0 + b0, SUBLANE), val)
        groups.append(val)
    return jnp.concatenate(groups, axis=0)


def _hgrn_kernel(*refs, L, nc, heads, dk, dv, layer, has_state):
    if has_state:
        q_ref, f_ref, v_ref, gt_ref, lbraw_ref, g_ref, s0_ref, y_ref, s_out, st, lb_scr, g_scr = refs
    else:
        q_ref, f_ref, v_ref, gt_ref, lbraw_ref, g_ref, y_ref, s_out, st, lb_scr, g_scr = refs
    c_idx = pl.program_id(1)

    @pl.when(c_idx == 0)
    def _init():
        for h in range(heads):
            st[h] = s0_ref[h].T if has_state else jnp.zeros((dv, dk), F32)
        raw = lbraw_ref[...]
        e = jnp.exp(raw - jnp.max(raw, axis=0, keepdims=True))
        sm = e / jnp.sum(e, axis=0, keepdims=True)
        lb = jnp.zeros((1, heads * dk), F32)
        for j in range(1, layer + 1):
            lb = lb + sm[j:j + 1, :]
        lb_scr[...] = lb

    row = lax.broadcasted_iota(jnp.int32, (L, L), 0)
    col = lax.broadcasted_iota(jnp.int32, (L, L), 1)
    tri = jnp.where(row >= col, 1.0, 0.0).astype(BF16)
    diag = row == col
    levels = []
    bs = 2
    while bs <= L:
        hs, lg = bs // 2, bs.bit_length() - 1
        pair = ((row >> lg) == (col >> lg)) & ((row & (bs - 1)) >= hs) & ((col & (bs - 1)) < hs)
        levels.append((bs, pair))
        bs *= 2

    def nt(a, b):
        return lax.dot_general(a, b, _NT, preferred_element_type=F32)

    def head_group(hs):
        kb16, qb16, gs, f16 = [], [], [], []
        for h in hs:
            kc = pl.ds(h * dk, dk)
            z = f_ref[:, kc]
            lb = lb_scr[:, kc]
            t = jnp.exp(_neg_abs(z))
            r = 1.0 / (1.0 + t)
            pos = z >= 0.0
            f = jnp.where(pos, 1.0 + lb * t, lb + t) * r
            kb = (1.0 - lb) * (jnp.where(pos, t, 1.0) * r)
            qb = _silu_tanh(q_ref[:, kc].astype(F32))
            kb16.append(kb.astype(BF16))
            qb16.append(qb.astype(BF16))
            f16.append(f.astype(BF16))
            gs.append(_dot(tri, jnp.concatenate(_split3(jnp.log(f) * LOG2E), axis=1)))
        g2, pend = [], []
        for u, h in enumerate(hs):
            g2.append(gs[u][:, :dk] + gs[u][:, dk:2 * dk] + gs[u][:, 2 * dk:])
            g_scr[h] = g2[u]
            pend.append(nt(qb16[u], kb16[u]))
        a = [0.0] * len(hs)
        mask = diag
        for bs, pair in levels:
            new = []
            for u, h in enumerate(hs):
                if bs == 2:
                    new.append(nt(qb16[u] * f16[u], kb16[u]))
                    continue
                e = jnp.exp2(_neg_abs(g2[u] - _ref_rows(g_scr.at[h], L, bs, dk))).astype(BF16)
                new.append(nt(qb16[u] * e, kb16[u] * e))
            a = [jnp.where(mask, p, x) for p, x in zip(pend, a)]
            pend, mask = new, pair
        a = [jnp.where(mask, p, x) for p, x in zip(pend, a)]
        o = []
        for u, h in enumerate(hs):
            vc = pl.ds(h * dv, dv)
            qg = qb16[u] * jnp.exp2(g2[u]).astype(BF16)
            o.append((_dot(a[u].astype(BF16), v_ref[:, vc]), nt(qg, st[h].astype(BF16))))
        upd = []
        for u, h in enumerate(hs):
            g_end = g2[u][L - 1:L, :]
            ke = kb16[u] * jnp.exp2(g_end - g2[u]).astype(BF16)
            upd.append(lax.dot_general(v_ref[:, pl.ds(h * dv, dv)], ke, _TN, preferred_element_type=F32))
        for u, h in enumerate(hs):
            vc = pl.ds(h * dv, dv)
            y = _rms(o[u][0] + o[u][1]) * g_ref[...] * _silu_tanh(gt_ref[:, vc].astype(F32))
            y_ref[:, vc] = y.astype(y_ref.dtype)
        for u, h in enumerate(hs):
            st[h] = jnp.exp2(g2[u][L - 1:L, :]) * st[h] + upd[u]

    for h0 in range(0, heads, HGRN_UNROLL):
        head_group(list(range(h0, h0 + HGRN_UNROLL)))

    @pl.when(c_idx == nc - 1)
    def _fin():
        for h in range(heads):
            s_out[h] = st[h].T


def _hgrn(p_f, q_off, f_off, gt_off, p_bf, v_off, lb_raw, g, s0, layer, batch, t):
    heads, dk, dv = B_HEADS, B_DK, B_DV
    nk, nv = heads * dk, heads * dv
    L = min(HGRN_CHUNK, t)
    nc = t // L
    assert t % L == 0 and q_off % nk == 0 and f_off % nk == 0 and gt_off % nv == 0 and v_off % nv == 0
    assert heads % HGRN_UNROLL == 0
    has_state = s0 is not None

    def rows(width, off):
        return pl.BlockSpec((L, width), lambda b, c: (b * nc + c, off // width))

    in_specs = [rows(nk, q_off), rows(nk, f_off), rows(nv, v_off), rows(nv, gt_off),
                pl.BlockSpec(lb_raw.shape, lambda b, c: (0, 0)),
                pl.BlockSpec((1, dv), lambda b, c: (0, 0))]
    args = [p_bf, p_f, p_bf, p_bf, lb_raw, g.reshape(1, dv)]
    if has_state:
        in_specs.append(pl.BlockSpec((None, None, heads, dk, dv), lambda b, c: (layer, b, 0, 0, 0)))
        args.append(s0)
    return pl.pallas_call(
        functools.partial(_hgrn_kernel, L=L, nc=nc, heads=heads, dk=dk, dv=dv, layer=layer,
                          has_state=has_state),
        grid=(batch, nc),
        in_specs=in_specs,
        out_specs=[pl.BlockSpec((L, nv), lambda b, c: (b * nc + c, 0)),
                   pl.BlockSpec((None, heads, dk, dv), lambda b, c: (b, 0, 0, 0))],
        out_shape=[jax.ShapeDtypeStruct((batch * t, nv), BF16),
                   jax.ShapeDtypeStruct((batch, heads, dk, dv), F32)],
        scratch_shapes=[pltpu.VMEM((heads, dv, dk), F32),
                        pltpu.VMEM((1, nk), F32),
                        pltpu.VMEM((heads, L, dk), F32)],
        compiler_params=_params("parallel", "arbitrary"),
        name="hgrn",
    )(*args)


def _prep_weights(w_in, w_proj_a, w_proj_b, w_proj_m, w_out, w_ffn_in, w_ffn_out, w_mem_kv):
    a_qk = 2 * A_HEADS * A_DQK
    a_v = A_HEADS * A_DV
    b_k = B_HEADS * B_DK
    b_v = B_HEADS * B_DV
    m_w = M_HEADS * M_HD
    sizes = [a_qk, a_v, a_v, 2 * A_HEADS, b_k, b_k, b_v, b_v, m_w]
    starts = [0]
    for s in sizes:
        starts.append(starts[-1] + s)
    qk_a, v_a, o_a, if_a, q_b, f_b, i_b, gt_b, q_m = [
        w_in[:, :, starts[i]:starts[i + 1]] for i in range(len(sizes))]
    gates = w_in[:, :, starts[-1]:]
    w_bf = jnp.concatenate([v_a, i_b, q_m, qk_a, o_a, q_b, gt_b, gates], axis=-1).astype(BF16)
    w_f = f_b.astype(BF16)
    w_if = jnp.pad(if_a, ((0, 0), (0, 0), (0, LANE - 2 * A_HEADS))).astype(BF16)
    offs, pos = {}, 0
    for name, width in (("v_a", a_v), ("i_b", b_v), ("q_m", m_w), ("qk_a", a_qk), ("o_a", a_v),
                        ("q_b", b_k), ("gt_b", b_v), ("gates", 0)):
        offs[name] = pos
        pos += width
    return dict(w_bf=w_bf, w_f=w_f, w_if=w_if, offs=offs,
                w_proj_a=w_proj_a.astype(BF16), w_proj_b=w_proj_b.astype(BF16),
                w_proj_m=w_proj_m.astype(BF16), w_out=w_out.astype(BF16),
                w_ffn_in=w_ffn_in.astype(BF16), w_ffn_out=w_ffn_out.astype(BF16),
                w_mem_kv=w_mem_kv.astype(BF16))


def _trunk(x, memk, memv, state, W, P):
    batch, t, d = x.shape
    offs = W["offs"]
    x2 = x.reshape(batch * t, d)
    h = _rmsnorm(x2, P["g_mix"][0], BF16)
    new_conv, new_c, new_n, new_m, new_s = [], [], [], [], []
    for l in range(DEPTH):
        p_bf = _matmul(h, W["w_bf"], l, BF16, tn=1024)
        p_f = _matmul(h, W["w_f"], l, F32)
        p_if = _matmul(h, W["w_if"], l, F32)

        ya, ca, na, ma, cbuf = _mlstm(
            p_bf, offs["qk_a"], offs["o_a"], p_bf, offs["v_a"], p_if, P["b_igate"][l], P["b_fgate"][l],
            P["conv_w"][l], P["conv_b"][l], P["g_mlstm"][l],
            None if state is None else state[:4], l, batch, t)
        yb, sb = _hgrn(p_f, offs["q_b"], 0, offs["gt_b"], p_bf, offs["i_b"], P["lb_raw"],
                       P["g_hgrn"][l], None if state is None else state[4], l, batch, t)
        ym = _memattn(p_bf, offs["q_m"], memk, memv, l, batch, t)

        mix = _mix(ya, yb, ym, W["w_proj_a"], W["w_proj_b"], W["w_proj_m"], l, p_bf, offs["gates"], d)
        x2, h2 = _mm_res_norm(mix, W["w_out"], l, x2, P["g_ffn"][l], BF16, tm=512)
        act = _ffn_in(h2, W["w_ffn_in"], l)
        last = l == DEPTH - 1
        x2, h = _mm_res_norm(act, W["w_ffn_out"], l, x2, P["g_final"] if last else P["g_mix"][l + 1],
                             F32 if last else BF16, tm=256)
        new_conv.append(cbuf)
        new_c.append(ca)
        new_n.append(na)
        new_m.append(ma)
        new_s.append(sb)
    return (h.reshape(batch, t, d), jnp.stack(new_conv), jnp.stack(new_c), jnp.stack(new_n),
            jnp.stack(new_m), jnp.stack(new_s))


def kernel(x_prompt, x_sample, mem_prompt, cache_mem_k, cache_mem_v, state_mlstm_c, state_mlstm_n, state_mlstm_m, state_mlstm_conv, state_hgrn_s, g_mix, w_in, conv_w, conv_b, b_igate, b_fgate, g_mlstm, lb_raw, g_hgrn, g_mem, w_mem_kv, w_proj_a, w_proj_b, w_proj_m, w_out, g_ffn, w_ffn_in, w_ffn_out, g_final):
    W = _prep_weights(w_in, w_proj_a, w_proj_b, w_proj_m, w_out, w_ffn_in, w_ffn_out, w_mem_kv)
    P = dict(g_mix=g_mix, conv_w=conv_w, conv_b=conv_b, b_igate=b_igate, b_fgate=b_fgate, g_mlstm=g_mlstm,
             lb_raw=lb_raw, g_hgrn=g_hgrn, g_ffn=g_ffn, g_final=g_final)
    batch, n_mem, d = mem_prompt.shape
    dec_batch = x_sample.shape[0]
    m_w = M_HEADS * M_HD

    mem2 = mem_prompt.reshape(batch * n_mem, d)
    ks, vs = [], []
    for l in range(DEPTH):
        mem_n = _rmsnorm(mem2, g_mem[l], BF16)
        ks.append(_matmul(mem_n, W["w_mem_kv"], l, F32, col_off=0, n_cols=m_w))
        vs.append(_matmul(mem_n, W["w_mem_kv"], l, F32, col_off=m_w, n_cols=m_w))
    mem_k = jnp.stack(ks).reshape(DEPTH, batch, n_mem, m_w)
    mem_v = jnp.stack(vs).reshape(DEPTH, batch, n_mem, m_w)

    y_p, conv_p, c_p, n_p, m_p, s_p = _trunk(x_prompt, mem_k, mem_v, None, W, P)

    state = (state_mlstm_c, state_mlstm_n, state_mlstm_m, state_mlstm_conv, state_hgrn_s)
    y_s, conv_s, c_s, n_s, m_s, s_s = _trunk(
        x_sample, cache_mem_k.reshape(DEPTH, dec_batch, n_mem, m_w),
        cache_mem_v.reshape(DEPTH, dec_batch, n_mem, m_w), state, W, P)

    mem_shape = (DEPTH, batch, n_mem, M_HEADS, M_HD)
    return (y_p, y_s, mem_k.reshape(mem_shape), mem_v.reshape(mem_shape), conv_p, c_p, n_p, m_p, s_p,
            conv_s, c_s, n_s, m_s, s_s)
```

```python
import functools

import jax
import jax.numpy as jnp
from jax import lax
from jax.experimental import pallas as pl
from jax.experimental.pallas import tpu as pltpu

F32 = jnp.float32
BF16 = jnp.bfloat16

EPS = 1e-6
NEG_BIG = -1e30
DEPTH = 4
CONV_W = 4
A_HEADS = 4
A_DQK = 256
A_DV = 512
B_HEADS = 16
B_DK = 128
B_DV = 128
M_HEADS = 4
M_HD = 512

LANE = 128
SUBLANE = 8
VMEM_LIMIT = 56 * 1024 * 1024
MLSTM_CHUNK = 128
MLSTM_GROUP = 4
HGRN_CHUNK = 64
HGRN_UNROLL = 4
LOG2E = 1.4426950408889634

_NT = (((1,), (1,)), ((), ()))
_TN = (((0,), (0,)), ((), ()))


def _params(*sem):
    return pltpu.CompilerParams(dimension_semantics=sem, vmem_limit_bytes=VMEM_LIMIT)


def _tile(n, pref, align):
    best = None
    t = align
    while t <= min(n, pref):
        if n % t == 0:
            best = t
        t += align
    return best if best is not None else n


def _dot(a, b):
    return jnp.dot(a, b, preferred_element_type=F32)


def _log_sigmoid(x):
    return jnp.minimum(x, 0.0) - jnp.log1p(jnp.exp(-jnp.abs(x)))


def _silu(x):
    return x * jax.nn.sigmoid(x)


def _silu_tanh(x):
    hx = 0.5 * x
    return hx + hx * jnp.tanh(hx)


def _rms(x):
    return x * lax.rsqrt(jnp.mean(x * x, axis=-1, keepdims=True) + EPS)


def _rmsnorm_kernel(x_ref, g_ref, o_ref):
    o_ref[...] = (_rms(x_ref[...]) * g_ref[...]).astype(o_ref.dtype)


def _rmsnorm(x, g, out_dtype):
    m, d = x.shape
    tm = _tile(m, 512, 16)
    return pl.pallas_call(
        _rmsnorm_kernel,
        grid=(m // tm,),
        in_specs=[pl.BlockSpec((tm, d), lambda i: (i, 0)), pl.BlockSpec((1, d), lambda i: (0, 0))],
        out_specs=pl.BlockSpec((tm, d), lambda i: (i, 0)),
        out_shape=jax.ShapeDtypeStruct((m, d), out_dtype),
        compiler_params=_params("parallel"),
        name="rmsnorm",
    )(x, g.reshape(1, d))


def _matmul_kernel(a_ref, w_ref, o_ref):
    o_ref[...] = _dot(a_ref[...], w_ref[...]).astype(o_ref.dtype)


def _matmul(a, w, layer, out_dtype, *, col_off=0, n_cols=None, tm=2048, tn=512):
    m, k = a.shape
    n_cols = w.shape[2] - col_off if n_cols is None else n_cols
    tm = _tile(m, tm, 16)
    tn = _tile(n_cols, tn, LANE)
    assert col_off % tn == 0
    joff = col_off // tn
    return pl.pallas_call(
        _matmul_kernel,
        grid=(m // tm, n_cols // tn),
        in_specs=[pl.BlockSpec((tm, k), lambda i, j: (i, 0)),
                  pl.BlockSpec((None, k, tn), lambda i, j: (layer, 0, j + joff))],
        out_specs=pl.BlockSpec((tm, tn), lambda i, j: (i, j)),
        out_shape=jax.ShapeDtypeStruct((m, n_cols), out_dtype),
        compiler_params=_params("parallel", "arbitrary"),
        name="matmul",
    )(a, w)


def _matmul_side_kernel(a_ref, w_ref, ws_ref, o_ref, os_ref):
    a = a_ref[...]
    o_ref[...] = _dot(a, w_ref[...])

    @pl.when(pl.program_id(1) == 0)
    def _():
        os_ref[...] = _dot(a, ws_ref[...])


def _matmul_side(a, w, ws, layer, *, tm=2048, tn=512):
    m, k = a.shape
    n, ns = w.shape[2], ws.shape[2]
    tm = _tile(m, tm, 16)
    tn = _tile(n, tn, LANE)
    return pl.pallas_call(
        _matmul_side_kernel,
        grid=(m // tm, n // tn),
        in_specs=[pl.BlockSpec((tm, k), lambda i, j: (i, 0)),
                  pl.BlockSpec((None, k, tn), lambda i, j: (layer, 0, j)),
                  pl.BlockSpec((None, k, ns), lambda i, j: (layer, 0, 0))],
        out_specs=[pl.BlockSpec((tm, tn), lambda i, j: (i, j)),
                   pl.BlockSpec((tm, ns), lambda i, j: (i, 0))],
        out_shape=[jax.ShapeDtypeStruct((m, n), F32), jax.ShapeDtypeStruct((m, ns), F32)],
        compiler_params=_params("parallel", "arbitrary"),
        name="matmul_side",
    )(a, w, ws)


def _mm_res_norm_kernel(a_ref, w_ref, x_ref, g_ref, xo_ref, ho_ref):
    x = x_ref[...] + _dot(a_ref[...], w_ref[...])
    xo_ref[...] = x
    ho_ref[...] = (_rms(x) * g_ref[...]).astype(ho_ref.dtype)


def _mm_res_norm(a, w, layer, x, g, norm_dtype, *, tm):
    m, k = a.shape
    d = w.shape[2]
    tm = _tile(m, tm, 16)
    return pl.pallas_call(
        _mm_res_norm_kernel,
        grid=(m // tm,),
        in_specs=[pl.BlockSpec((tm, k), lambda i: (i, 0)),
                  pl.BlockSpec((None, k, d), lambda i: (layer, 0, 0), pipeline_mode=pl.Buffered(1)),
                  pl.BlockSpec((tm, d), lambda i: (i, 0)),
                  pl.BlockSpec((1, d), lambda i: (0, 0))],
        out_specs=[pl.BlockSpec((tm, d), lambda i: (i, 0)),
                   pl.BlockSpec((tm, d), lambda i: (i, 0))],
        out_shape=[jax.ShapeDtypeStruct((m, d), F32), jax.ShapeDtypeStruct((m, d), norm_dtype)],
        compiler_params=_params("parallel"),
        name="mm_res_norm",
    )(a, w, x, g.reshape(1, d))


def _ffn_in_kernel(h_ref, wg_ref, wu_ref, o_ref):
    h = h_ref[...]
    g = _dot(h, wg_ref[...])
    u = _dot(h, wu_ref[...])
    o_ref[...] = (_silu_tanh(g) * u).astype(o_ref.dtype)


def _ffn_in(h, w, layer, *, tm=1024, tn=512):
    m, d = h.shape
    dff = w.shape[2] // 2
    tm = _tile(m, tm, 16)
    tn = _tile(dff, tn, LANE)
    nj = dff // tn
    return pl.pallas_call(
        _ffn_in_kernel,
        grid=(m // tm, nj),
        in_specs=[pl.BlockSpec((tm, d), lambda i, j: (i, 0)),
                  pl.BlockSpec((None, d, tn), lambda i, j: (layer, 0, j)),
                  pl.BlockSpec((None, d, tn), lambda i, j: (layer, 0, j + nj))],
        out_specs=pl.BlockSpec((tm, tn), lambda i, j: (i, j)),
        out_shape=jax.ShapeDtypeStruct((m, dff), BF16),
        compiler_params=_params("parallel", "arbitrary"),
        name="ffn_in",
    )(h, w, w)


def _mix_kernel(ya_ref, yb_ref, ym_ref, wa_ref, wb_ref, wm_ref, ga_ref, gb_ref, gm_ref, o_ref):
    def gate(ref):
        return jax.nn.sigmoid(ref[...].astype(F32))

    acc = gate(ga_ref) * _dot(ya_ref[...], wa_ref[...])
    acc = acc + gate(gb_ref) * _dot(yb_ref[...], wb_ref[...])
    acc = acc + gate(gm_ref) * _dot(ym_ref[...], wm_ref[...])
    o_ref[...] = acc.astype(o_ref.dtype)


def _mix(ya, yb, ym, wa, wb, wm, layer, p_f, gate_off, d, *, tm=1024, tn=256):
    m = ya.shape[0]
    tm = _tile(m, tm, 16)
    tn = _tile(d, tn, LANE)
    assert gate_off % tn == 0
    goff = gate_off // tn
    nj = d // tn

    def y_spec(y):
        return pl.BlockSpec((tm, y.shape[1]), lambda i, j: (i, 0))

    def w_spec(w):
        return pl.BlockSpec((None, w.shape[1], tn), lambda i, j: (layer, 0, j))

    def g_spec(which):
        return pl.BlockSpec((tm, tn), lambda i, j: (i, goff + which * nj + j))

    return pl.pallas_call(
        _mix_kernel,
        grid=(m // tm, nj),
        in_specs=[y_spec(ya), y_spec(yb), y_spec(ym), w_spec(wa), w_spec(wb), w_spec(wm),
                  g_spec(0), g_spec(1), g_spec(2)],
        out_specs=pl.BlockSpec((tm, tn), lambda i, j: (i, j)),
        out_shape=jax.ShapeDtypeStruct((m, d), BF16),
        compiler_params=_params("parallel", "arbitrary"),
        name="mix",
    )(ya, yb, ym, wa, wb, wm, p_f, p_f, p_f)


def _memattn_kernel(q_ref, k_ref, v_ref, o_ref, k16, v16, *, heads, hd):
    @pl.when(pl.program_id(1) == 0)
    def _():
        for h in range(heads):
            k16[h] = k_ref[:, h, :].astype(BF16)
            v16[h] = v_ref[:, h, :].astype(BF16)

    for h in range(heads):
        sl = slice(h * hd, (h + 1) * hd)
        s = lax.dot_general(q_ref[:, sl], k16[h], _NT, preferred_element_type=F32) * (hd ** -0.5)
        p = jnp.exp(s - jnp.max(s, axis=-1, keepdims=True))
        p = p / jnp.sum(p, axis=-1, keepdims=True)
        o_ref[:, sl] = _dot(p.astype(BF16), v16[h]).astype(o_ref.dtype)


def _memattn(p_bf, q_off, memk, memv, layer, batch, t, *, tq=512):
    n_mem, heads, hd = memk.shape[2:]
    w = heads * hd
    tq = _tile(t, tq, 16)
    nt = t // tq
    assert q_off % w == 0
    qoff = q_off // w
    return pl.pallas_call(
        functools.partial(_memattn_kernel, heads=heads, hd=hd),
        grid=(batch, nt),
        in_specs=[pl.BlockSpec((tq, w), lambda b, i: (b * nt + i, qoff)),
                  pl.BlockSpec((None, None, n_mem, heads, hd), lambda b, i: (layer, b, 0, 0, 0)),
                  pl.BlockSpec((None, None, n_mem, heads, hd), lambda b, i: (layer, b, 0, 0, 0))],
        out_specs=pl.BlockSpec((tq, w), lambda b, i: (b * nt + i, 0)),
        out_shape=jax.ShapeDtypeStruct((batch * t, w), BF16),
        scratch_shapes=[pltpu.VMEM((heads, n_mem, hd), BF16), pltpu.VMEM((heads, n_mem, hd), BF16)],
        compiler_params=_params("parallel", "arbitrary"),
        name="memattn",
    )(p_bf, memk, memv)


def _split3(x):
    hi = x.astype(BF16)
    r = x - hi.astype(F32)
    mid = r.astype(BF16)
    lo = (r - mid.astype(F32)).astype(BF16)
    return hi, mid, lo


def _mlstm_kernel(*refs, L, nc, heads, dqk, dv, has_state):
    if has_state:
        (qk_ref, v_ref, og_ref, ifc_ref, ifr_ref, bc_ref, br_ref, cw_ref, cb_ref, g_ref,
         c0_ref, n0_ref, m0_ref, conv0_ref,
         y_ref, c_out, n_out, m_out, conv_out, ubuf, caug, m_scr, corr_scr) = refs
    else:
        (qk_ref, v_ref, og_ref, ifc_ref, ifr_ref, bc_ref, br_ref, cw_ref, cb_ref, g_ref,
         y_ref, c_out, n_out, m_out, conv_out, ubuf, caug, m_scr, corr_scr) = refs
    c_idx = pl.program_id(1)
    nq = heads * dqk

    @pl.when(c_idx == 0)
    def _init():
        if has_state:
            ubuf[...] = conv0_ref[...]
            caug[:, :, :dv] = c0_ref[...]
            caug[:, :, dv:] = n0_ref[...]
            m_scr[...] = m0_ref[...]
        else:
            ubuf[...] = jnp.zeros((SUBLANE, 2 * nq), F32)
            caug[...] = jnp.zeros(caug.shape, F32)
            m_scr[...] = jnp.zeros(m_scr.shape, F32)

    pre_c = ifc_ref[...] + bc_ref[...]
    pre_r = ifr_ref[...] + br_ref[...]
    lf_c = _log_sigmoid(pre_c)
    lf_r = _log_sigmoid(pre_r)
    row = lax.broadcasted_iota(jnp.int32, (L, L), 0)
    col = lax.broadcasted_iota(jnp.int32, (L, L), 1)
    causal = row >= col
    tri = jnp.where(causal, 1.0, 0.0).astype(BF16)
    tri_t = jnp.where(row <= col, 1.0, 0.0).astype(BF16)
    fcum_c = sum(_dot(tri, part) for part in _split3(lf_c))
    fcum_r = sum(_dot(part, tri_t) for part in _split3(lf_r))

    shift = jnp.concatenate(
        [jnp.where(row - col == CONV_W - 1 - j, 1.0, 0.0).astype(BF16) for j in range(CONV_W - 1)], axis=0)
    sub = lax.broadcasted_iota(jnp.int32, (SUBLANE, 2 * nq), 0)
    corr = jnp.zeros((SUBLANE, 2 * nq), F32)
    for t0 in range(CONV_W - 1):
        r = sum(cw_ref[j:j + 1, :] * ubuf[SUBLANE - (CONV_W - 1) + t0 + j:SUBLANE - (CONV_W - 2) + t0 + j, :]
                for j in range(CONV_W - 1 - t0))
        corr = jnp.where(sub == t0, r, corr)
    corr_scr[...] = corr

    ones_blk = jnp.where(lax.broadcasted_iota(jnp.int32, (L, LANE), 1) == 0, 1.0, 0.0).astype(BF16)

    def conv(c0, scale):
        sh = _dot(shift, qk_ref[:, c0:c0 + dqk])
        blocks = []
        for b0 in range(0, dqk, LANE):
            cs = slice(c0 + b0, c0 + b0 + LANE)
            acc = cb_ref[:, cs] + cw_ref[CONV_W - 1:CONV_W, cs] * qk_ref[:, cs].astype(F32)
            for j in range(CONV_W - 1):
                acc = acc + cw_ref[j:j + 1, cs] * sh[j * L:(j + 1) * L, b0:b0 + LANE]
            acc = jnp.concatenate([acc[:SUBLANE] + corr_scr[:, cs], acc[SUBLANE:]], axis=0)
            act = _silu_tanh(acc)
            blocks.append((act if scale is None else act * scale).astype(BF16))
        return jnp.concatenate(blocks, axis=1)

    def head_group(hs):
        q16 = {h: conv(h * dqk, None) for h in hs}
        k16 = {h: conv(nq + h * dqk, dqk ** -0.5) for h in hs}
        qc = {h: _dot(q16[h], caug[h].astype(BF16)) for h in hs}
        s_raw = {h: lax.dot_general(q16[h], k16[h], _NT, preferred_element_type=F32) for h in hs}

        m_t, w_state, w_end, decay, s16 = {}, {}, {}, {}, {}
        for h in hs:
            fc_c = fcum_c[:, heads + h:heads + h + 1]
            fc_r = fcum_r[heads + h:heads + h + 1, :]
            li_c = pre_c[:, h:h + 1]
            li_r = pre_r[h:h + 1, :]
            m_prev = m_scr[h, 0:1, 0:1]
            d = jnp.where(causal, fc_c - fc_r + li_r, NEG_BIG)
            inter = fc_c + m_prev
            m_t[h] = jnp.maximum(inter, jnp.max(d, axis=1, keepdims=True))
            w_state[h] = jnp.exp(inter - m_t[h])
            s16[h] = (s_raw[h] * jnp.exp(d - m_t[h])).astype(BF16)
            m_new = m_t[h][L - 1:L, :]
            fc_last = fc_c[L - 1:L, :]
            decay[h] = jnp.exp(fc_last + m_prev - m_new)
            w_end[h] = jnp.exp(fc_last - fc_c + li_c - m_new)
            m_scr[h] = jnp.broadcast_to(m_new, (SUBLANE, LANE))

        def v_of(h):
            return v_ref[:, h * dv:(h + 1) * dv]

        num = {h: _dot(s16[h], v_of(h)) for h in hs}
        dsum = {h: _dot(s16[h], ones_blk) for h in hs}
        kw = {h: k16[h] * w_end[h].astype(BF16) for h in hs}

        for h in hs:
            den = dsum[h][:, 0:1] + w_state[h] * qc[h][:, dv:dv + 1]
            rden = 1.0 / jnp.maximum(jnp.abs(den), jnp.exp(-m_t[h]))
            hb, ssq = [], 0.0
            for b0 in range(0, dv, LANE):
                blk = (num[h][:, b0:b0 + LANE] + w_state[h] * qc[h][:, b0:b0 + LANE]) * rden
                ssq = ssq + jnp.sum(blk * blk, axis=1, keepdims=True)
                hb.append(blk)
            rinv = lax.rsqrt(ssq * (1.0 / dv) + EPS)
            for i, b0 in enumerate(range(0, dv, LANE)):
                vs = slice(h * dv + b0, h * dv + b0 + LANE)
                gate = 0.5 + 0.5 * jnp.tanh(0.5 * og_ref[:, vs].astype(F32))
                y_ref[:, vs] = (hb[i] * rinv * g_ref[:, vs] * gate).astype(y_ref.dtype)

        for h in hs:
            caug[h, :, :dv] = decay[h] * caug[h, :, :dv] + lax.dot_general(
                kw[h], v_of(h), _TN, preferred_element_type=F32)
            caug[h, :, dv:] = decay[h] * caug[h, :, dv:] + lax.dot_general(
                kw[h], ones_blk, _TN, preferred_element_type=F32)

    for h0 in range(0, heads, MLSTM_GROUP):
        head_group(list(range(h0, min(h0 + MLSTM_GROUP, heads))))

    ubuf[...] = qk_ref[L - 2 * SUBLANE:L, :].astype(F32)[SUBLANE:, :]

    @pl.when(c_idx == nc - 1)
    def _fin():
        c_out[...] = caug[:, :, :dv]
        n_out[...] = caug[:, :, dv:]
        m_out[...] = m_scr[...]
        conv_out[...] = ubuf[...]


def _mlstm(p_f, qk_off, og_off, p_bf, v_off, p_if, b_i, b_f, conv_w, conv_b, g, state, layer, batch, t):
    heads, dqk, dv = A_HEADS, A_DQK, A_DV
    nq, nv = heads * dqk, heads * dv
    m_rows = batch * t
    L = min(MLSTM_CHUNK, t)
    nc = t // L
    assert t % L == 0 and L >= 2 * SUBLANE and qk_off % (2 * nq) == 0 and og_off % nv == 0 and v_off % nv == 0
    has_state = state is not None

    ifr = p_if[:, :SUBLANE].reshape(m_rows // L, L, SUBLANE).transpose(0, 2, 1)
    bias = jnp.concatenate([b_i, b_f]).astype(F32)
    bias_c = jnp.zeros((1, LANE), F32).at[0, :2 * heads].set(bias)
    bias_r = jnp.zeros((SUBLANE, 1), F32).at[:2 * heads, 0].set(bias)

    def rows(width, off):
        return pl.BlockSpec((L, width), lambda b, c: (b * nc + c, off // width))

    def whole(shape):
        return pl.BlockSpec(shape, lambda b, c: (0,) * len(shape))

    in_specs = [rows(2 * nq, qk_off), rows(nv, v_off), rows(nv, og_off), rows(LANE, 0),
                pl.BlockSpec((None, SUBLANE, L), lambda b, c: (b * nc + c, 0, 0)),
                whole((1, LANE)), whole((SUBLANE, 1)), whole((CONV_W, 2 * nq)), whole((1, 2 * nq)),
                whole((1, nv))]
    args = [p_f, p_bf, p_f, p_if, ifr, bias_c, bias_r, conv_w, conv_b.reshape(1, 2 * nq), g.reshape(1, nv)]
    if has_state:
        c0, n0, m0, conv0 = state
        n0p = jnp.pad(n0[layer][..., None], ((0, 0), (0, 0), (0, 0), (0, LANE - 1)))
        m0p = jnp.broadcast_to(m0[layer][:, :, None, None], (batch, heads, SUBLANE, LANE))
        conv0p = jnp.pad(conv0[layer], ((0, 0), (SUBLANE - (CONV_W - 1), 0), (0, 0)))
        in_specs += [pl.BlockSpec((None, None, heads, dqk, dv), lambda b, c: (layer, b, 0, 0, 0)),
                     pl.BlockSpec((None, heads, dqk, LANE), lambda b, c: (b, 0, 0, 0)),
                     pl.BlockSpec((None, heads, SUBLANE, LANE), lambda b, c: (b, 0, 0, 0)),
                     pl.BlockSpec((None, SUBLANE, 2 * nq), lambda b, c: (b, 0, 0))]
        args += [c0, n0p, m0p, conv0p]

    out_specs = [pl.BlockSpec((L, nv), lambda b, c: (b * nc + c, 0)),
                 pl.BlockSpec((None, heads, dqk, dv), lambda b, c: (b, 0, 0, 0)),
                 pl.BlockSpec((None, heads, dqk, LANE), lambda b, c: (b, 0, 0, 0)),
                 pl.BlockSpec((None, heads, SUBLANE, LANE), lambda b, c: (b, 0, 0, 0)),
                 pl.BlockSpec((None, SUBLANE, 2 * nq), lambda b, c: (b, 0, 0))]
    out_shape = [jax.ShapeDtypeStruct((m_rows, nv), BF16),
                 jax.ShapeDtypeStruct((batch, heads, dqk, dv), F32),
                 jax.ShapeDtypeStruct((batch, heads, dqk, LANE), F32),
                 jax.ShapeDtypeStruct((batch, heads, SUBLANE, LANE), F32),
                 jax.ShapeDtypeStruct((batch, SUBLANE, 2 * nq), F32)]
    ya, c_new, n_new, m_new, conv_new = pl.pallas_call(
        functools.partial(_mlstm_kernel, L=L, nc=nc, heads=heads, dqk=dqk, dv=dv, has_state=has_state),
        grid=(batch, nc),
        in_specs=in_specs,
        out_specs=out_specs,
        out_shape=out_shape,
        scratch_shapes=[pltpu.VMEM((SUBLANE, 2 * nq), F32),
                        pltpu.VMEM((heads, dqk, dv + LANE), F32),
                        pltpu.VMEM((heads, SUBLANE, LANE), F32),
                        pltpu.VMEM((SUBLANE, 2 * nq), F32)],
        compiler_params=_params("parallel", "arbitrary"),
        name="mlstm",
    )(*args)
    return ya, c_new, n_new[..., 0], m_new[:, :, 0, 0], conv_new[:, SUBLANE - (CONV_W - 1):, :]


def _neg_abs(x):
    bits = lax.bitcast_convert_type(x, jnp.uint32) | jnp.uint32(0x80000000)
    return lax.bitcast_convert_type(bits, F32)


def _ref_rows(g_scr, L, bs, n):
    hs = bs // 2

    def bcast(r, rows):
        return jnp.broadcast_to(g_scr[r:r + 1, :], (rows, n))

    if bs >= 2 * SUBLANE:
        return jnp.concatenate([bcast(b0 + hs - 1, bs) for b0 in range(0, L, bs)], axis=0)
    sub = lax.broadcasted_iota(jnp.int32, (SUBLANE, n), 0)
    groups = []
    for g0 in range(0, L, SUBLANE):
        r0 = g0 + hs - 1
        val = bcast(r0, SUBLANE)
        for b0 in range(bs, SUBLANE, bs):
            val = jnp.where(sub >= b0, bcast(r0 + b0, SUBLANE), val)
        groups.append(val)
    return jnp.concatenate(groups, axis=0)


def _hgrn_kernel(*refs, L, nc, heads, dk, dv, layer, has_state):
    if has_state:
        q_ref, f_ref, v_ref, gt_ref, lbraw_ref, g_ref, s0_ref, y_ref, s_out, st, lb_scr, g_scr = refs
    else:
        q_ref, f_ref, v_ref, gt_ref, lbraw_ref, g_ref, y_ref, s_out, st, lb_scr, g_scr = refs
    c_idx = pl.program_id(1)

    @pl.when(c_idx == 0)
    def _init():
        for h in range(heads):
            st[h] = s0_ref[h].T if has_state else jnp.zeros((dv, dk), F32)
        raw = lbraw_ref[...]
        e = jnp.exp(raw - jnp.max(raw, axis=0, keepdims=True))
        sm = e / jnp.sum(e, axis=0, keepdims=True)
        lb = jnp.zeros((1, heads * dk), F32)
        for j in range(1, layer + 1):
            lb = lb + sm[j:j + 1, :]
        lb_scr[...] = lb

    row = lax.broadcasted_iota(jnp.int32, (L, L), 0)
    col = lax.broadcasted_iota(jnp.int32, (L, L), 1)
    tri = jnp.where(row >= col, 1.0, 0.0).astype(BF16)
    diag = row == col
    levels = []
    bs = 2
    while bs <= L:
        hs, lg = bs // 2, bs.bit_length() - 1
        pair = ((row >> lg) == (col >> lg)) & ((row & (bs - 1)) >= hs) & ((col & (bs - 1)) < hs)
        levels.append((bs, pair))
        bs *= 2

    def nt(a, b):
        return lax.dot_general(a, b, _NT, preferred_element_type=F32)

    def head_group(hs):
        kb16, qb16, gs, f16 = [], [], [], []
        for h in hs:
            kc = pl.ds(h * dk, dk)
            z = f_ref[:, kc]
            lb = lb_scr[:, kc]
            t = jnp.exp(_neg_abs(z))
            r = 1.0 / (1.0 + t)
            pos = z >= 0.0
            f = jnp.where(pos, 1.0 + lb * t, lb + t) * r
            kb = (1.0 - lb) * (jnp.where(pos, t, 1.0) * r)
            qb = _silu_tanh(q_ref[:, kc].astype(F32))
            kb16.append(kb.astype(BF16))
            qb16.append(qb.astype(BF16))
            f16.append(f.astype(BF16))
            gs.append(_dot(tri, jnp.concatenate(_split3(jnp.log(f) * LOG2E), axis=1)))
        g2, pend = [], []
        for u, h in enumerate(hs):
            g2.append(gs[u][:, :dk] + gs[u][:, dk:2 * dk] + gs[u][:, 2 * dk:])
            g_scr[h] = g2[u]
            pend.append(nt(qb16[u], kb16[u]))
        a = [0.0] * len(hs)
        mask = diag
        for bs, pair in levels:
            new = []
            for u, h in enumerate(hs):
                if bs == 2:
                    new.append(nt(qb16[u] * f16[u], kb16[u]))
                    continue
                e = jnp.exp2(_neg_abs(g2[u] - _ref_rows(g_scr.at[h], L, bs, dk))).astype(BF16)
                new.append(nt(qb16[u] * e, kb16[u] * e))
            a = [jnp.where(mask, p, x) for p, x in zip(pend, a)]
            pend, mask = new, pair
        a = [jnp.where(mask, p, x) for p, x in zip(pend, a)]
        o = []
        for u, h in enumerate(hs):
            vc = pl.ds(h * dv, dv)
            qg = qb16[u] * jnp.exp2(g2[u]).astype(BF16)
            o.append((_dot(a[u].astype(BF16), v_ref[:, vc]), nt(qg, st[h].astype(BF16))))
        upd = []
        for u, h in enumerate(hs):
            g_end = g2[u][L - 1:L, :]
            ke = kb16[u] * jnp.exp2(g_end - g2[u]).astype(BF16)
            upd.append(lax.dot_general(v_ref[:, pl.ds(h * dv, dv)], ke, _TN, preferred_element_type=F32))
        for u, h in enumerate(hs):
            vc = pl.ds(h * dv, dv)
            y = _rms(o[u][0] + o[u][1]) * g_ref[...] * _silu_tanh(gt_ref[:, vc].astype(F32))
            y_ref[:, vc] = y.astype(y_ref.dtype)
        for u, h in enumerate(hs):
            st[h] = jnp.exp2(g2[u][L - 1:L, :]) * st[h] + upd[u]

    for h0 in range(0, heads, HGRN_UNROLL):
        head_group(list(range(h0, h0 + HGRN_UNROLL)))

    @pl.when(c_idx == nc - 1)
    def _fin():
        for h in range(heads):
            s_out[h] = st[h].T


def _hgrn(p_f, q_off, f_off, gt_off, p_bf, v_off, lb_raw, g, s0, layer, batch, t):
    heads, dk, dv = B_HEADS, B_DK, B_DV
    nk, nv = heads * dk, heads * dv
    L = min(HGRN_CHUNK, t)
    nc = t // L
    assert t % L == 0 and q_off % nk == 0 and f_off % nk == 0 and gt_off % nv == 0 and v_off % nv == 0
    assert heads % HGRN_UNROLL == 0
    has_state = s0 is not None

    def rows(width, off):
        return pl.BlockSpec((L, width), lambda b, c: (b * nc + c, off // width))

    in_specs = [rows(nk, q_off), rows(nk, f_off), rows(nv, v_off), rows(nv, gt_off),
                pl.BlockSpec(lb_raw.shape, lambda b, c: (0, 0)),
                pl.BlockSpec((1, dv), lambda b, c: (0, 0))]
    args = [p_bf, p_f, p_bf, p_bf, lb_raw, g.reshape(1, dv)]
    if has_state:
        in_specs.append(pl.BlockSpec((None, None, heads, dk, dv), lambda b, c: (layer, b, 0, 0, 0)))
        args.append(s0)
    return pl.pallas_call(
        functools.partial(_hgrn_kernel, L=L, nc=nc, heads=heads, dk=dk, dv=dv, layer=layer,
                          has_state=has_state),
        grid=(batch, nc),
        in_specs=in_specs,
        out_specs=[pl.BlockSpec((L, nv), lambda b, c: (b * nc + c, 0)),
                   pl.BlockSpec((None, heads, dk, dv), lambda b, c: (b, 0, 0, 0))],
        out_shape=[jax.ShapeDtypeStruct((batch * t, nv), BF16),
                   jax.ShapeDtypeStruct((batch, heads, dk, dv), F32)],
        scratch_shapes=[pltpu.VMEM((heads, dv, dk), F32),
                        pltpu.VMEM((1, nk), F32),
                        pltpu.VMEM((heads, L, dk), F32)],
        compiler_params=_params("parallel", "arbitrary"),
        name="hgrn",
    )(*args)


def _prep_weights(w_in, w_proj_a, w_proj_b, w_proj_m, w_out, w_ffn_in, w_ffn_out, w_mem_kv):
    a_qk = 2 * A_HEADS * A_DQK
    a_v = A_HEADS * A_DV
    b_k = B_HEADS * B_DK
    b_v = B_HEADS * B_DV
    m_w = M_HEADS * M_HD
    sizes = [a_qk, a_v, a_v, 2 * A_HEADS, b_k, b_k, b_v, b_v, m_w]
    starts = [0]
    for s in sizes:
        starts.append(starts[-1] + s)
    qk_a, v_a, o_a, if_a, q_b, f_b, i_b, gt_b, q_m = [
        w_in[:, :, starts[i]:starts[i + 1]] for i in range(len(sizes))]
    gates = w_in[:, :, starts[-1]:]
    w_bf = jnp.concatenate([v_a, i_b, q_m, qk_a, o_a, q_b, gt_b, gates], axis=-1).astype(BF16)
    w_f = f_b.astype(BF16)
    w_if = jnp.pad(if_a, ((0, 0), (0, 0), (0, LANE - 2 * A_HEADS))).astype(BF16)
    offs, pos = {}, 0
    for name, width in (("v_a", a_v), ("i_b", b_v), ("q_m", m_w), ("qk_a", a_qk), ("o_a", a_v),
                        ("q_b", b_k), ("gt_b", b_v), ("gates", 0)):
        offs[name] = pos
        pos += width
    return dict(w_bf=w_bf, w_f=w_f, w_if=w_if, offs=offs,
                w_proj_a=w_proj_a.astype(BF16), w_proj_b=w_proj_b.astype(BF16),
                w_proj_m=w_proj_m.astype(BF16), w_out=w_out.astype(BF16),
                w_ffn_in=w_ffn_in.astype(BF16), w_ffn_out=w_ffn_out.astype(BF16),
                w_mem_kv=w_mem_kv.astype(BF16))


def _trunk(x, memk, memv, state, W, P):
    batch, t, d = x.shape
    offs = W["offs"]
    x2 = x.reshape(batch * t, d)
    h = _rmsnorm(x2, P["g_mix"][0], BF16)
    new_conv, new_c, new_n, new_m, new_s = [], [], [], [], []
    for l in range(DEPTH):
        p_bf = _matmul(h, W["w_bf"], l, BF16, tn=1024)
        p_f, p_if = _matmul_side(h, W["w_f"], W["w_if"], l)

        ya, ca, na, ma, cbuf = _mlstm(
            p_bf, offs["qk_a"], offs["o_a"], p_bf, offs["v_a"], p_if, P["b_igate"][l], P["b_fgate"][l],
            P["conv_w"][l], P["conv_b"][l], P["g_mlstm"][l],
            None if state is None else state[:4], l, batch, t)
        yb, sb = _hgrn(p_f, offs["q_b"], 0, offs["gt_b"], p_bf, offs["i_b"], P["lb_raw"],
                       P["g_hgrn"][l], None if state is None else state[4], l, batch, t)
        ym = _memattn(p_bf, offs["q_m"], memk, memv, l, batch, t)

        mix = _mix(ya, yb, ym, W["w_proj_a"], W["w_proj_b"], W["w_proj_m"], l, p_bf, offs["gates"], d)
        x2, h2 = _mm_res_norm(mix, W["w_out"], l, x2, P["g_ffn"][l], BF16, tm=512)
        act = _ffn_in(h2, W["w_ffn_in"], l)
        last = l == DEPTH - 1
        x2, h = _mm_res_norm(act, W["w_ffn_out"], l, x2, P["g_final"] if last else P["g_mix"][l + 1],
                             F32 if last else BF16, tm=256)
        new_conv.append(cbuf)
        new_c.append(ca)
        new_n.append(na)
        new_m.append(ma)
        new_s.append(sb)
    return (h.reshape(batch, t, d), jnp.stack(new_conv), jnp.stack(new_c), jnp.stack(new_n),
            jnp.stack(new_m), jnp.stack(new_s))


def kernel(x_prompt, x_sample, mem_prompt, cache_mem_k, cache_mem_v, state_mlstm_c, state_mlstm_n, state_mlstm_m, state_mlstm_conv, state_hgrn_s, g_mix, w_in, conv_w, conv_b, b_igate, b_fgate, g_mlstm, lb_raw, g_hgrn, g_mem, w_mem_kv, w_proj_a, w_proj_b, w_proj_m, w_out, g_ffn, w_ffn_in, w_ffn_out, g_final):
    W = _prep_weights(w_in, w_proj_a, w_proj_b, w_proj_m, w_out, w_ffn_in, w_ffn_out, w_mem_kv)
    P = dict(g_mix=g_mix, conv_w=conv_w, conv_b=conv_b, b_igate=b_igate, b_fgate=b_fgate, g_mlstm=g_mlstm,
             lb_raw=lb_raw, g_hgrn=g_hgrn, g_ffn=g_ffn, g_final=g_final)
    batch, n_mem, d = mem_prompt.shape
    dec_batch = x_sample.shape[0]
    m_w = M_HEADS * M_HD

    mem2 = mem_prompt.reshape(batch * n_mem, d)
    ks, vs = [], []
    for l in range(DEPTH):
        mem_n = _rmsnorm(mem2, g_mem[l], BF16)
        ks.append(_matmul(mem_n, W["w_mem_kv"], l, F32, col_off=0, n_cols=m_w))
        vs.append(_matmul(mem_n, W["w_mem_kv"], l, F32, col_off=m_w, n_cols=m_w))
    mem_shape = (DEPTH, batch, n_mem, M_HEADS, M_HD)
    mem_k = jnp.stack(ks).reshape(mem_shape)
    mem_v = jnp.stack(vs).reshape(mem_shape)

    y_p, conv_p, c_p, n_p, m_p, s_p = _trunk(x_prompt, mem_k, mem_v, None, W, P)

    state = (state_mlstm_c, state_mlstm_n, state_mlstm_m, state_mlstm_conv, state_hgrn_s)
    y_s, conv_s, c_s, n_s, m_s, s_s = _trunk(x_sample, cache_mem_k, cache_mem_v, state, W, P)

    return (y_p, y_s, mem_k, mem_v, conv_p, c_p, n_p, m_p, s_p, conv_s, c_s, n_s, m_s, s_s)
```

```python
import functools

import jax
import jax.numpy as jnp
from jax import lax
from jax.experimental import pallas as pl
from jax.experimental.pallas import tpu as pltpu

F32 = jnp.float32
BF16 = jnp.bfloat16

EPS = 1e-6
NEG_BIG = -1e30
DEPTH = 4
CONV_W = 4
A_HEADS = 4
A_DQK = 256
A_DV = 512
B_HEADS = 16
B_DK = 128
B_DV = 128
M_HEADS = 4
M_HD = 512

LANE = 128
SUBLANE = 8
VMEM_LIMIT = 56 * 1024 * 1024
MLSTM_CHUNK = 128
MLSTM_GROUP = 4
HGRN_CHUNK = 64
HGRN_UNROLL = 4
LOG2E = 1.4426950408889634

_NT = (((1,), (1,)), ((), ()))
_TN = (((0,), (0,)), ((), ()))


def _params(*sem):
    return pltpu.CompilerParams(dimension_semantics=sem, vmem_limit_bytes=VMEM_LIMIT)


def _tile(n, pref, align):
    best = None
    t = align
    while t <= min(n, pref):
        if n % t == 0:
            best = t
        t += align
    return best if best is not None else n


def _dot(a, b):
    return jnp.dot(a, b, preferred_element_type=F32)


def _log_sigmoid(x):
    return jnp.minimum(x, 0.0) - jnp.log1p(jnp.exp(-jnp.abs(x)))


def _silu(x):
    return x * jax.nn.sigmoid(x)


def _silu_tanh(x):
    hx = 0.5 * x
    return hx + hx * jnp.tanh(hx)


def _rms(x):
    return x * lax.rsqrt(jnp.mean(x * x, axis=-1, keepdims=True) + EPS)


def _rmsnorm_kernel(x_ref, g_ref, o_ref):
    o_ref[...] = (_rms(x_ref[...]) * g_ref[...]).astype(o_ref.dtype)


def _rmsnorm(x, g, out_dtype):
    m, d = x.shape
    tm = _tile(m, 512, 16)
    return pl.pallas_call(
        _rmsnorm_kernel,
        grid=(m // tm,),
        in_specs=[pl.BlockSpec((tm, d), lambda i: (i, 0)), pl.BlockSpec((1, d), lambda i: (0, 0))],
        out_specs=pl.BlockSpec((tm, d), lambda i: (i, 0)),
        out_shape=jax.ShapeDtypeStruct((m, d), out_dtype),
        compiler_params=_params("parallel"),
        name="rmsnorm",
    )(x, g.reshape(1, d))


def _matmul_kernel(a_ref, w_ref, o_ref):
    o_ref[...] = _dot(a_ref[...], w_ref[...]).astype(o_ref.dtype)


def _matmul(a, w, layer, out_dtype, *, col_off=0, n_cols=None, tm=2048, tn=512):
    m, k = a.shape
    n_cols = w.shape[2] - col_off if n_cols is None else n_cols
    tm = _tile(m, tm, 16)
    tn = _tile(n_cols, tn, LANE)
    assert col_off % tn == 0
    joff = col_off // tn
    return pl.pallas_call(
        _matmul_kernel,
        grid=(m // tm, n_cols // tn),
        in_specs=[pl.BlockSpec((tm, k), lambda i, j: (i, 0)),
                  pl.BlockSpec((None, k, tn), lambda i, j: (layer, 0, j + joff))],
        out_specs=pl.BlockSpec((tm, tn), lambda i, j: (i, j)),
        out_shape=jax.ShapeDtypeStruct((m, n_cols), out_dtype),
        compiler_params=_params("parallel", "arbitrary"),
        name="matmul",
    )(a, w)


def _matmul_side_kernel(a_ref, w_ref, ws_ref, o_ref, os_ref):
    a = a_ref[...]
    o_ref[...] = _dot(a, w_ref[...])

    @pl.when(pl.program_id(1) == 0)
    def _():
        os_ref[...] = _dot(a, ws_ref[...])


def _matmul_side(a, w, ws, layer, *, tm=2048, tn=512):
    m, k = a.shape
    n, ns = w.shape[2], ws.shape[2]
    tm = _tile(m, tm, 16)
    tn = _tile(n, tn, LANE)
    return pl.pallas_call(
        _matmul_side_kernel,
        grid=(m // tm, n // tn),
        in_specs=[pl.BlockSpec((tm, k), lambda i, j: (i, 0)),
                  pl.BlockSpec((None, k, tn), lambda i, j: (layer, 0, j)),
                  pl.BlockSpec((None, k, ns), lambda i, j: (layer, 0, 0))],
        out_specs=[pl.BlockSpec((tm, tn), lambda i, j: (i, j)),
                   pl.BlockSpec((tm, ns), lambda i, j: (i, 0))],
        out_shape=[jax.ShapeDtypeStruct((m, n), F32), jax.ShapeDtypeStruct((m, ns), F32)],
        compiler_params=_params("parallel", "arbitrary"),
        name="matmul_side",
    )(a, w, ws)


def _mm_res_norm_kernel(a_ref, w_ref, x_ref, g_ref, xo_ref, ho_ref):
    x = x_ref[...] + _dot(a_ref[...], w_ref[...])
    xo_ref[...] = x
    ho_ref[...] = (_rms(x) * g_ref[...]).astype(ho_ref.dtype)


def _mm_res_norm(a, w, layer, x, g, norm_dtype, *, tm):
    m, k = a.shape
    d = w.shape[2]
    tm = _tile(m, tm, 16)
    return pl.pallas_call(
        _mm_res_norm_kernel,
        grid=(m // tm,),
        in_specs=[pl.BlockSpec((tm, k), lambda i: (i, 0)),
                  pl.BlockSpec((None, k, d), lambda i: (layer, 0, 0), pipeline_mode=pl.Buffered(1)),
                  pl.BlockSpec((tm, d), lambda i: (i, 0)),
                  pl.BlockSpec((1, d), lambda i: (0, 0))],
        out_specs=[pl.BlockSpec((tm, d), lambda i: (i, 0)),
                   pl.BlockSpec((tm, d), lambda i: (i, 0))],
        out_shape=[jax.ShapeDtypeStruct((m, d), F32), jax.ShapeDtypeStruct((m, d), norm_dtype)],
        compiler_params=_params("parallel"),
        name="mm_res_norm",
    )(a, w, x, g.reshape(1, d))


def _ffn_in_kernel(h_ref, wg_ref, wu_ref, o_ref):
    h = h_ref[...]
    g = _dot(h, wg_ref[...])
    u = _dot(h, wu_ref[...])
    o_ref[...] = (_silu_tanh(g) * u).astype(o_ref.dtype)


def _ffn_in(h, w, layer, *, tm=1024, tn=512):
    m, d = h.shape
    dff = w.shape[2] // 2
    tm = _tile(m, tm, 16)
    tn = _tile(dff, tn, LANE)
    nj = dff // tn
    return pl.pallas_call(
        _ffn_in_kernel,
        grid=(m // tm, nj),
        in_specs=[pl.BlockSpec((tm, d), lambda i, j: (i, 0)),
                  pl.BlockSpec((None, d, tn), lambda i, j: (layer, 0, j)),
                  pl.BlockSpec((None, d, tn), lambda i, j: (layer, 0, j + nj))],
        out_specs=pl.BlockSpec((tm, tn), lambda i, j: (i, j)),
        out_shape=jax.ShapeDtypeStruct((m, dff), BF16),
        compiler_params=_params("parallel", "arbitrary"),
        name="ffn_in",
    )(h, w, w)


def _mix_kernel(ya_ref, yb_ref, ym_ref, wa_ref, wb_ref, wm_ref, ga_ref, gb_ref, gm_ref, o_ref):
    def gate(ref):
        return 0.5 + 0.5 * jnp.tanh(0.5 * ref[...].astype(F32))

    acc = gate(ga_ref) * _dot(ya_ref[...], wa_ref[...])
    acc = acc + gate(gb_ref) * _dot(yb_ref[...], wb_ref[...])
    acc = acc + gate(gm_ref) * _dot(ym_ref[...], wm_ref[...])
    o_ref[...] = acc.astype(o_ref.dtype)


def _mix(ya, yb, ym, wa, wb, wm, layer, p_f, gate_off, d, *, tm=1024, tn=512):
    m = ya.shape[0]
    tm = _tile(m, tm, 16)
    tn = _tile(d, tn, LANE)
    assert gate_off % tn == 0
    goff = gate_off // tn
    nj = d // tn

    def y_spec(y):
        return pl.BlockSpec((tm, y.shape[1]), lambda i, j: (i, 0))

    def w_spec(w):
        return pl.BlockSpec((None, w.shape[1], tn), lambda i, j: (layer, 0, j))

    def g_spec(which):
        return pl.BlockSpec((tm, tn), lambda i, j: (i, goff + which * nj + j))

    return pl.pallas_call(
        _mix_kernel,
        grid=(m // tm, nj),
        in_specs=[y_spec(ya), y_spec(yb), y_spec(ym), w_spec(wa), w_spec(wb), w_spec(wm),
                  g_spec(0), g_spec(1), g_spec(2)],
        out_specs=pl.BlockSpec((tm, tn), lambda i, j: (i, j)),
        out_shape=jax.ShapeDtypeStruct((m, d), BF16),
        compiler_params=_params("parallel", "arbitrary"),
        name="mix",
    )(ya, yb, ym, wa, wb, wm, p_f, p_f, p_f)


def _memattn_kernel(q_ref, k_ref, v_ref, o_ref, k16, v16, *, heads, hd):
    @pl.when(pl.program_id(1) == 0)
    def _():
        for h in range(heads):
            k16[h] = k_ref[:, h, :].astype(BF16)
            v16[h] = v_ref[:, h, :].astype(BF16)

    for h in range(heads):
        sl = slice(h * hd, (h + 1) * hd)
        s = lax.dot_general(q_ref[:, sl], k16[h], _NT, preferred_element_type=F32) * (hd ** -0.5)
        p = jnp.exp(s - jnp.max(s, axis=-1, keepdims=True))
        p = p / jnp.sum(p, axis=-1, keepdims=True)
        o_ref[:, sl] = _dot(p.astype(BF16), v16[h]).astype(o_ref.dtype)


def _memattn(p_bf, q_off, memk, memv, layer, batch, t, *, tq=512):
    n_mem, heads, hd = memk.shape[2:]
    w = heads * hd
    tq = _tile(t, tq, 16)
    nt = t // tq
    assert q_off % w == 0
    qoff = q_off // w
    return pl.pallas_call(
        functools.partial(_memattn_kernel, heads=heads, hd=hd),
        grid=(batch, nt),
        in_specs=[pl.BlockSpec((tq, w), lambda b, i: (b * nt + i, qoff)),
                  pl.BlockSpec((None, None, n_mem, heads, hd), lambda b, i: (layer, b, 0, 0, 0)),
                  pl.BlockSpec((None, None, n_mem, heads, hd), lambda b, i: (layer, b, 0, 0, 0))],
        out_specs=pl.BlockSpec((tq, w), lambda b, i: (b * nt + i, 0)),
        out_shape=jax.ShapeDtypeStruct((batch * t, w), BF16),
        scratch_shapes=[pltpu.VMEM((heads, n_mem, hd), BF16), pltpu.VMEM((heads, n_mem, hd), BF16)],
        compiler_params=_params("parallel", "arbitrary"),
        name="memattn",
    )(p_bf, memk, memv)


def _split3(x):
    hi = x.astype(BF16)
    r = x - hi.astype(F32)
    mid = r.astype(BF16)
    lo = (r - mid.astype(F32)).astype(BF16)
    return hi, mid, lo


def _mlstm_kernel(*refs, L, nc, heads, dqk, dv, has_state, has_prev):
    if has_prev:
        refs = refs[1:]
    if has_state:
        (qk_ref, v_ref, og_ref, ifc_ref, ifr_ref, bc_ref, br_ref, cw_ref, cb_ref, g_ref,
         c0_ref, n0_ref, m0_ref, conv0_ref,
         y_ref, c_out, n_out, m_out, conv_out, ubuf, caug, m_scr, corr_scr) = refs
    else:
        (qk_ref, v_ref, og_ref, ifc_ref, ifr_ref, bc_ref, br_ref, cw_ref, cb_ref, g_ref,
         y_ref, c_out, n_out, m_out, conv_out, ubuf, caug, m_scr, corr_scr) = refs
    c_idx = pl.program_id(1)
    nq = heads * dqk

    @pl.when(c_idx == 0)
    def _init():
        if has_state:
            ubuf[...] = conv0_ref[...]
            caug[:, :, :dv] = c0_ref[...]
            caug[:, :, dv:] = n0_ref[...]
            m_scr[...] = m0_ref[...]
        else:
            ubuf[...] = jnp.zeros((SUBLANE, 2 * nq), F32)
            caug[...] = jnp.zeros(caug.shape, F32)
            m_scr[...] = jnp.zeros(m_scr.shape, F32)

    pre_c = ifc_ref[...] + bc_ref[...]
    pre_r = ifr_ref[...] + br_ref[...]
    lf_c = _log_sigmoid(pre_c)
    lf_r = _log_sigmoid(pre_r)
    row = lax.broadcasted_iota(jnp.int32, (L, L), 0)
    col = lax.broadcasted_iota(jnp.int32, (L, L), 1)
    causal = row >= col
    tri = jnp.where(causal, 1.0, 0.0).astype(BF16)
    tri_t = jnp.where(row <= col, 1.0, 0.0).astype(BF16)
    fcum_c = sum(_dot(tri, part) for part in _split3(lf_c))
    fcum_r = sum(_dot(part, tri_t) for part in _split3(lf_r))

    shift = jnp.concatenate(
        [jnp.where(row - col == CONV_W - 1 - j, 1.0, 0.0).astype(BF16) for j in range(CONV_W - 1)], axis=0)
    sub = lax.broadcasted_iota(jnp.int32, (SUBLANE, 2 * nq), 0)
    corr = jnp.zeros((SUBLANE, 2 * nq), F32)
    for t0 in range(CONV_W - 1):
        r = sum(cw_ref[j:j + 1, :] * ubuf[SUBLANE - (CONV_W - 1) + t0 + j:SUBLANE - (CONV_W - 2) + t0 + j, :]
                for j in range(CONV_W - 1 - t0))
        corr = jnp.where(sub == t0, r, corr)
    corr_scr[...] = corr

    ones_blk = jnp.where(lax.broadcasted_iota(jnp.int32, (L, LANE), 1) == 0, 1.0, 0.0).astype(BF16)

    def conv(c0, scale):
        sh = _dot(shift, qk_ref[:, c0:c0 + dqk])
        blocks = []
        for b0 in range(0, dqk, LANE):
            cs = slice(c0 + b0, c0 + b0 + LANE)
            acc = cb_ref[:, cs] + cw_ref[CONV_W - 1:CONV_W, cs] * qk_ref[:, cs].astype(F32)
            for j in range(CONV_W - 1):
                acc = acc + cw_ref[j:j + 1, cs] * sh[j * L:(j + 1) * L, b0:b0 + LANE]
            acc = jnp.concatenate([acc[:SUBLANE] + corr_scr[:, cs], acc[SUBLANE:]], axis=0)
            act = _silu_tanh(acc)
            blocks.append((act if scale is None else act * scale).astype(BF16))
        return jnp.concatenate(blocks, axis=1)

    def head_group(hs):
        q16 = {h: conv(h * dqk, None) for h in hs}
        k16 = {h: conv(nq + h * dqk, dqk ** -0.5) for h in hs}
        qc = {h: _dot(q16[h], caug[h].astype(BF16)) for h in hs}
        s_raw = {h: lax.dot_general(q16[h], k16[h], _NT, preferred_element_type=F32) for h in hs}

        m_t, w_state, w_end, decay, s16 = {}, {}, {}, {}, {}
        for h in hs:
            fc_c = fcum_c[:, heads + h:heads + h + 1]
            fc_r = fcum_r[heads + h:heads + h + 1, :]
            li_c = pre_c[:, h:h + 1]
            li_r = pre_r[h:h + 1, :]
            m_prev = m_scr[h, 0:1, 0:1]
            d = jnp.where(causal, fc_c - fc_r + li_r, NEG_BIG)
            inter = fc_c + m_prev
            m_t[h] = jnp.maximum(inter, jnp.max(d, axis=1, keepdims=True))
            w_state[h] = jnp.exp(inter - m_t[h])
            s16[h] = (s_raw[h] * jnp.exp(d - m_t[h])).astype(BF16)
            m_new = m_t[h][L - 1:L, :]
            fc_last = fc_c[L - 1:L, :]
            decay[h] = jnp.exp(fc_last + m_prev - m_new)
            w_end[h] = jnp.exp(fc_last - fc_c + li_c - m_new)
            m_scr[h] = jnp.broadcast_to(m_new, (SUBLANE, LANE))

        def v_of(h):
            return v_ref[:, h * dv:(h + 1) * dv]

        num = {h: _dot(s16[h], v_of(h)) for h in hs}
        dsum = {h: _dot(s16[h], ones_blk) for h in hs}
        kw = {h: k16[h] * w_end[h].astype(BF16) for h in hs}

        for h in hs:
            den = dsum[h][:, 0:1] + w_state[h] * qc[h][:, dv:dv + 1]
            rden = 1.0 / jnp.maximum(jnp.abs(den), jnp.exp(-m_t[h]))
            hb, ssq = [], 0.0
            for b0 in range(0, dv, LANE):
                blk = (num[h][:, b0:b0 + LANE] + w_state[h] * qc[h][:, b0:b0 + LANE]) * rden
                ssq = ssq + jnp.sum(blk * blk, axis=1, keepdims=True)
                hb.append(blk)
            rinv = lax.rsqrt(ssq * (1.0 / dv) + EPS)
            for i, b0 in enumerate(range(0, dv, LANE)):
                vs = slice(h * dv + b0, h * dv + b0 + LANE)
                gate = 0.5 + 0.5 * jnp.tanh(0.5 * og_ref[:, vs].astype(F32))
                y_ref[:, vs] = (hb[i] * rinv * g_ref[:, vs] * gate).astype(y_ref.dtype)

        for h in hs:
            caug[h, :, :dv] = decay[h] * caug[h, :, :dv] + lax.dot_general(
                kw[h], v_of(h), _TN, preferred_element_type=F32)
            caug[h, :, dv:] = decay[h] * caug[h, :, dv:] + lax.dot_general(
                kw[h], ones_blk, _TN, preferred_element_type=F32)

    for h0 in range(0, heads, MLSTM_GROUP):
        head_group(list(range(h0, min(h0 + MLSTM_GROUP, heads))))

    ubuf[...] = qk_ref[L - 2 * SUBLANE:L, :].astype(F32)[SUBLANE:, :]

    @pl.when(c_idx == nc - 1)
    def _fin():
        c_out[...] = caug[:, :, :dv]
        n_out[...] = caug[:, :, dv:]
        m_out[...] = m_scr[...]
        conv_out[...] = ubuf[...]


def _mlstm(p_f, qk_off, og_off, p_bf, v_off, p_if, b_i, b_f, conv_w, conv_b, g, state, c_acc, layer, batch, t):
    heads, dqk, dv = A_HEADS, A_DQK, A_DV
    nq, nv = heads * dqk, heads * dv
    m_rows = batch * t
    L = min(MLSTM_CHUNK, t)
    nc = t // L
    assert t % L == 0 and L >= 2 * SUBLANE and qk_off % (2 * nq) == 0 and og_off % nv == 0 and v_off % nv == 0
    has_state = state is not None

    ifr = p_if[:, :SUBLANE].reshape(m_rows // L, L, SUBLANE).transpose(0, 2, 1)
    bias = jnp.concatenate([b_i, b_f]).astype(F32)
    bias_c = jnp.zeros((1, LANE), F32).at[0, :2 * heads].set(bias)
    bias_r = jnp.zeros((SUBLANE, 1), F32).at[:2 * heads, 0].set(bias)

    def rows(width, off):
        return pl.BlockSpec((L, width), lambda b, c: (b * nc + c, off // width))

    def whole(shape):
        return pl.BlockSpec(shape, lambda b, c: (0,) * len(shape))

    in_specs = [rows(2 * nq, qk_off), rows(nv, v_off), rows(nv, og_off), rows(LANE, 0),
                pl.BlockSpec((None, SUBLANE, L), lambda b, c: (b * nc + c, 0, 0)),
                whole((1, LANE)), whole((SUBLANE, 1)), whole((CONV_W, 2 * nq)), whole((1, 2 * nq)),
                whole((1, nv))]
    args = [p_f, p_bf, p_f, p_if, ifr, bias_c, bias_r, conv_w, conv_b.reshape(1, 2 * nq), g.reshape(1, nv)]
    if has_state:
        c0, n0, m0, conv0 = state
        n0p = jnp.pad(n0[layer][..., None], ((0, 0), (0, 0), (0, 0), (0, LANE - 1)))
        m0p = jnp.broadcast_to(m0[layer][:, :, None, None], (batch, heads, SUBLANE, LANE))
        conv0p = jnp.pad(conv0[layer], ((0, 0), (SUBLANE - (CONV_W - 1), 0), (0, 0)))
        in_specs += [pl.BlockSpec((None, None, heads, dqk, dv), lambda b, c: (layer, b, 0, 0, 0)),
                     pl.BlockSpec((None, heads, dqk, LANE), lambda b, c: (b, 0, 0, 0)),
                     pl.BlockSpec((None, heads, SUBLANE, LANE), lambda b, c: (b, 0, 0, 0)),
                     pl.BlockSpec((None, SUBLANE, 2 * nq), lambda b, c: (b, 0, 0))]
        args += [c0, n0p, m0p, conv0p]

    aliases = {}
    if c_acc is not None:
        in_specs = [pl.BlockSpec(memory_space=pl.ANY)] + in_specs
        args = [c_acc] + args
        aliases = {0: 1}
    out_specs = [pl.BlockSpec((L, nv), lambda b, c: (b * nc + c, 0)),
                 pl.BlockSpec((None, None, heads, dqk, dv), lambda b, c: (layer, b, 0, 0, 0)),
                 pl.BlockSpec((None, heads, dqk, LANE), lambda b, c: (b, 0, 0, 0)),
                 pl.BlockSpec((None, heads, SUBLANE, LANE), lambda b, c: (b, 0, 0, 0)),
                 pl.BlockSpec((None, SUBLANE, 2 * nq), lambda b, c: (b, 0, 0))]
    out_shape = [jax.ShapeDtypeStruct((m_rows, nv), BF16),
                 jax.ShapeDtypeStruct((DEPTH, batch, heads, dqk, dv), F32),
                 jax.ShapeDtypeStruct((batch, heads, dqk, LANE), F32),
                 jax.ShapeDtypeStruct((batch, heads, SUBLANE, LANE), F32),
                 jax.ShapeDtypeStruct((batch, SUBLANE, 2 * nq), F32)]
    ya, c_new, n_new, m_new, conv_new = pl.pallas_call(
        functools.partial(_mlstm_kernel, L=L, nc=nc, heads=heads, dqk=dqk, dv=dv, has_state=has_state,
                          has_prev=c_acc is not None),
        grid=(batch, nc),
        in_specs=in_specs,
        out_specs=out_specs,
        out_shape=out_shape,
        input_output_aliases=aliases,
        scratch_shapes=[pltpu.VMEM((SUBLANE, 2 * nq), F32),
                        pltpu.VMEM((heads, dqk, dv + LANE), F32),
                        pltpu.VMEM((heads, SUBLANE, LANE), F32),
                        pltpu.VMEM((SUBLANE, 2 * nq), F32)],
        compiler_params=_params("parallel", "arbitrary"),
        name="mlstm",
    )(*args)
    return ya, c_new, n_new[..., 0], m_new[:, :, 0, 0], conv_new[:, SUBLANE - (CONV_W - 1):, :]


def _neg_abs(x):
    bits = lax.bitcast_convert_type(x, jnp.uint32) | jnp.uint32(0x80000000)
    return lax.bitcast_convert_type(bits, F32)


def _ref_rows(g_scr, L, bs, n):
    hs = bs // 2

    def bcast(r, rows):
        return jnp.broadcast_to(g_scr[r:r + 1, :], (rows, n))

    if bs >= 2 * SUBLANE:
        return jnp.concatenate([bcast(b0 + hs - 1, bs) for b0 in range(0, L, bs)], axis=0)
    sub = lax.broadcasted_iota(jnp.int32, (SUBLANE, n), 0)
    groups = []
    for g0 in range(0, L, SUBLANE):
        r0 = g0 + hs - 1
        val = bcast(r0, SUBLANE)
        for b0 in range(bs, SUBLANE, bs):
            val = jnp.where(sub >= b0, bcast(r0 + b0, SUBLANE), val)
        groups.append(val)
    return jnp.concatenate(groups, axis=0)


def _hgrn_kernel(*refs, L, nc, heads, dk, dv, layer, has_state, has_prev):
    if has_prev:
        refs = refs[1:]
    if has_state:
        q_ref, f_ref, v_ref, gt_ref, lbraw_ref, g_ref, s0_ref, y_ref, s_out, st, lb_scr, g_scr = refs
    else:
        q_ref, f_ref, v_ref, gt_ref, lbraw_ref, g_ref, y_ref, s_out, st, lb_scr, g_scr = refs
    c_idx = pl.program_id(1)

    @pl.when(c_idx == 0)
    def _init():
        for h in range(heads):
            st[h] = s0_ref[h].T if has_state else jnp.zeros((dv, dk), F32)
        raw = lbraw_ref[...]
        e = jnp.exp(raw - jnp.max(raw, axis=0, keepdims=True))
        sm = e / jnp.sum(e, axis=0, keepdims=True)
        lb = jnp.zeros((1, heads * dk), F32)
        for j in range(1, layer + 1):
            lb = lb + sm[j:j + 1, :]
        lb_scr[...] = lb

    row = lax.broadcasted_iota(jnp.int32, (L, L), 0)
    col = lax.broadcasted_iota(jnp.int32, (L, L), 1)
    tri = jnp.where(row >= col, 1.0, 0.0).astype(BF16)
    diag = row == col
    levels = []
    bs = 2
    while bs <= L:
        hs, lg = bs // 2, bs.bit_length() - 1
        pair = ((row >> lg) == (col >> lg)) & ((row & (bs - 1)) >= hs) & ((col & (bs - 1)) < hs)
        levels.append((bs, pair))
        bs *= 2

    def nt(a, b):
        return lax.dot_general(a, b, _NT, preferred_element_type=F32)

    def head_group(hs):
        kb16, qb16, gs, f16 = [], [], [], []
        for h in hs:
            kc = pl.ds(h * dk, dk)
            z = f_ref[:, kc]
            lb = lb_scr[:, kc]
            t = jnp.exp(_neg_abs(z))
            r = 1.0 / (1.0 + t)
            pos = z >= 0.0
            f = jnp.where(pos, 1.0 + lb * t, lb + t) * r
            kb = (1.0 - lb) * (jnp.where(pos, t, 1.0) * r)
            qb = _silu_tanh(q_ref[:, kc].astype(F32))
            kb16.append(kb.astype(BF16))
            qb16.append(qb.astype(BF16))
            f16.append(f.astype(BF16))
            gs.append(_dot(tri, jnp.concatenate(_split3(jnp.log(f) * LOG2E), axis=1)))
        g2, pend = [], []
        for u, h in enumerate(hs):
            g2.append(gs[u][:, :dk] + gs[u][:, dk:2 * dk] + gs[u][:, 2 * dk:])
            g_scr[h] = g2[u]
            pend.append(nt(qb16[u], kb16[u]))
        a = [0.0] * len(hs)
        mask = diag
        for bs, pair in levels:
            new = []
            for u, h in enumerate(hs):
                if bs == 2:
                    new.append(nt(qb16[u] * f16[u], kb16[u]))
                    continue
                e = jnp.exp2(_neg_abs(g2[u] - _ref_rows(g_scr.at[h], L, bs, dk))).astype(BF16)
                new.append(nt(qb16[u] * e, kb16[u] * e))
            a = [jnp.where(mask, p, x) for p, x in zip(pend, a)]
            pend, mask = new, pair
        a = [jnp.where(mask, p, x) for p, x in zip(pend, a)]
        o = []
        for u, h in enumerate(hs):
            vc = pl.ds(h * dv, dv)
            qg = qb16[u] * jnp.exp2(g2[u]).astype(BF16)
            o.append((_dot(a[u].astype(BF16), v_ref[:, vc]), nt(qg, st[h].astype(BF16))))
        upd = []
        for u, h in enumerate(hs):
            g_end = g2[u][L - 1:L, :]
            ke = kb16[u] * jnp.exp2(g_end - g2[u]).astype(BF16)
            upd.append(lax.dot_general(v_ref[:, pl.ds(h * dv, dv)], ke, _TN, preferred_element_type=F32))
        for u, h in enumerate(hs):
            vc = pl.ds(h * dv, dv)
            y = _rms(o[u][0] + o[u][1]) * g_ref[...] * _silu_tanh(gt_ref[:, vc].astype(F32))
            y_ref[:, vc] = y.astype(y_ref.dtype)
        for u, h in enumerate(hs):
            st[h] = jnp.exp2(g2[u][L - 1:L, :]) * st[h] + upd[u]

    for h0 in range(0, heads, HGRN_UNROLL):
        head_group(list(range(h0, h0 + HGRN_UNROLL)))

    @pl.when(c_idx == nc - 1)
    def _fin():
        for h in range(heads):
            s_out[h] = st[h].T


def _hgrn(p_f, q_off, f_off, gt_off, p_bf, v_off, lb_raw, g, s0, s_acc, layer, batch, t):
    heads, dk, dv = B_HEADS, B_DK, B_DV
    nk, nv = heads * dk, heads * dv
    L = min(HGRN_CHUNK, t)
    nc = t // L
    assert t % L == 0 and q_off % nk == 0 and f_off % nk == 0 and gt_off % nv == 0 and v_off % nv == 0
    assert heads % HGRN_UNROLL == 0
    has_state = s0 is not None

    def rows(width, off):
        return pl.BlockSpec((L, width), lambda b, c: (b * nc + c, off // width))

    in_specs = [rows(nk, q_off), rows(nk, f_off), rows(nv, v_off), rows(nv, gt_off),
                pl.BlockSpec(lb_raw.shape, lambda b, c: (0, 0)),
                pl.BlockSpec((1, dv), lambda b, c: (0, 0))]
    args = [p_bf, p_f, p_bf, p_bf, lb_raw, g.reshape(1, dv)]
    if has_state:
        in_specs.append(pl.BlockSpec((None, None, heads, dk, dv), lambda b, c: (layer, b, 0, 0, 0)))
        args.append(s0)
    aliases = {}
    if s_acc is not None:
        in_specs = [pl.BlockSpec(memory_space=pl.ANY)] + in_specs
        args = [s_acc] + args
        aliases = {0: 1}
    return pl.pallas_call(
        functools.partial(_hgrn_kernel, L=L, nc=nc, heads=heads, dk=dk, dv=dv, layer=layer,
                          has_state=has_state, has_prev=s_acc is not None),
        grid=(batch, nc),
        in_specs=in_specs,
        out_specs=[pl.BlockSpec((L, nv), lambda b, c: (b * nc + c, 0)),
                   pl.BlockSpec((None, None, heads, dk, dv), lambda b, c: (layer, b, 0, 0, 0))],
        out_shape=[jax.ShapeDtypeStruct((batch * t, nv), BF16),
                   jax.ShapeDtypeStruct((DEPTH, batch, heads, dk, dv), F32)],
        input_output_aliases=aliases,
        scratch_shapes=[pltpu.VMEM((heads, dv, dk), F32),
                        pltpu.VMEM((1, nk), F32),
                        pltpu.VMEM((heads, L, dk), F32)],
        compiler_params=_params("parallel", "arbitrary"),
        name="hgrn",
    )(*args)


def _prep_weights(w_in, w_proj_a, w_proj_b, w_proj_m, w_out, w_ffn_in, w_ffn_out, w_mem_kv):
    a_qk = 2 * A_HEADS * A_DQK
    a_v = A_HEADS * A_DV
    b_k = B_HEADS * B_DK
    b_v = B_HEADS * B_DV
    m_w = M_HEADS * M_HD
    sizes = [a_qk, a_v, a_v, 2 * A_HEADS, b_k, b_k, b_v, b_v, m_w]
    starts = [0]
    for s in sizes:
        starts.append(starts[-1] + s)
    qk_a, v_a, o_a, if_a, q_b, f_b, i_b, gt_b, q_m = [
        w_in[:, :, starts[i]:starts[i + 1]] for i in range(len(sizes))]
    gates = w_in[:, :, starts[-1]:]
    w_bf = jnp.concatenate([v_a, i_b, q_m, qk_a, o_a, q_b, gt_b, gates], axis=-1).astype(BF16)
    w_f = f_b.astype(BF16)
    w_if = jnp.pad(if_a, ((0, 0), (0, 0), (0, LANE - 2 * A_HEADS))).astype(BF16)
    offs, pos = {}, 0
    for name, width in (("v_a", a_v), ("i_b", b_v), ("q_m", m_w), ("qk_a", a_qk), ("o_a", a_v),
                        ("q_b", b_k), ("gt_b", b_v), ("gates", 0)):
        offs[name] = pos
        pos += width
    return dict(w_bf=w_bf, w_f=w_f, w_if=w_if, offs=offs,
                w_proj_a=w_proj_a.astype(BF16), w_proj_b=w_proj_b.astype(BF16),
                w_proj_m=w_proj_m.astype(BF16), w_out=w_out.astype(BF16),
                w_ffn_in=w_ffn_in.astype(BF16), w_ffn_out=w_ffn_out.astype(BF16),
                w_mem_kv=w_mem_kv.astype(BF16))


def _trunk(x, memk, memv, state, W, P):
    batch, t, d = x.shape
    offs = W["offs"]
    x2 = x.reshape(batch * t, d)
    h = _rmsnorm(x2, P["g_mix"][0], BF16)
    new_conv, new_n, new_m = [], [], []
    ca = sb = None
    for l in range(DEPTH):
        p_bf = _matmul(h, W["w_bf"], l, BF16, tn=1024)
        p_f, p_if = _matmul_side(h, W["w_f"], W["w_if"], l)

        ya, ca, na, ma, cbuf = _mlstm(
            p_bf, offs["qk_a"], offs["o_a"], p_bf, offs["v_a"], p_if, P["b_igate"][l], P["b_fgate"][l],
            P["conv_w"][l], P["conv_b"][l], P["g_mlstm"][l],
            None if state is None else state[:4], ca, l, batch, t)
        yb, sb = _hgrn(p_f, offs["q_b"], 0, offs["gt_b"], p_bf, offs["i_b"], P["lb_raw"],
                       P["g_hgrn"][l], None if state is None else state[4], sb, l, batch, t)
        ym = _memattn(p_bf, offs["q_m"], memk, memv, l, batch, t)

        mix = _mix(ya, yb, ym, W["w_proj_a"], W["w_proj_b"], W["w_proj_m"], l, p_bf, offs["gates"], d)
        x2, h2 = _mm_res_norm(mix, W["w_out"], l, x2, P["g_ffn"][l], BF16, tm=512)
        act = _ffn_in(h2, W["w_ffn_in"], l)
        last = l == DEPTH - 1
        x2, h = _mm_res_norm(act, W["w_ffn_out"], l, x2, P["g_final"] if last else P["g_mix"][l + 1],
                             F32 if last else BF16, tm=256)
        new_conv.append(cbuf)
        new_n.append(na)
        new_m.append(ma)
    return (h.reshape(batch, t, d), jnp.stack(new_conv), ca, jnp.stack(new_n), jnp.stack(new_m), sb)


def kernel(x_prompt, x_sample, mem_prompt, cache_mem_k, cache_mem_v, state_mlstm_c, state_mlstm_n, state_mlstm_m, state_mlstm_conv, state_hgrn_s, g_mix, w_in, conv_w, conv_b, b_igate, b_fgate, g_mlstm, lb_raw, g_hgrn, g_mem, w_mem_kv, w_proj_a, w_proj_b, w_proj_m, w_out, g_ffn, w_ffn_in, w_ffn_out, g_final):
    W = _prep_weights(w_in, w_proj_a, w_proj_b, w_proj_m, w_out, w_ffn_in, w_ffn_out, w_mem_kv)
    P = dict(g_mix=g_mix, conv_w=conv_w, conv_b=conv_b, b_igate=b_igate, b_fgate=b_fgate, g_mlstm=g_mlstm,
             lb_raw=lb_raw, g_hgrn=g_hgrn, g_ffn=g_ffn, g_final=g_final)
    batch, n_mem, d = mem_prompt.shape
    dec_batch = x_sample.shape[0]
    m_w = M_HEADS * M_HD

    mem2 = mem_prompt.reshape(batch * n_mem, d)
    ks, vs = [], []
    for l in range(DEPTH):
        mem_n = _rmsnorm(mem2, g_mem[l], BF16)
        ks.append(_matmul(mem_n, W["w_mem_kv"], l, F32, col_off=0, n_cols=m_w))
        vs.append(_matmul(mem_n, W["w_mem_kv"], l, F32, col_off=m_w, n_cols=m_w))
    mem_shape = (DEPTH, batch, n_mem, M_HEADS, M_HD)
    mem_k = jnp.stack(ks).reshape(mem_shape)
    mem_v = jnp.stack(vs).reshape(mem_shape)

    y_p, conv_p, c_p, n_p, m_p, s_p = _trunk(x_prompt, mem_k, mem_v, None, W, P)

    state = (state_mlstm_c, state_mlstm_n, state_mlstm_m, state_mlstm_conv, state_hgrn_s)
    y_s, conv_s, c_s, n_s, m_s, s_s = _trunk(x_sample, cache_mem_k, cache_mem_v, state, W, P)

    return (y_p, y_s, mem_k, mem_v, conv_p, c_p, n_p, m_p, s_p, conv_s, c_s, n_s, m_s, s_s)
```

```python
import functools

import jax
import jax.numpy as jnp
from jax import lax
from jax.experimental import pallas as pl
from jax.experimental.pallas import tpu as pltpu

F32 = jnp.float32
BF16 = jnp.bfloat16

EPS = 1e-6
NEG_BIG = -1e30
DEPTH = 4
CONV_W = 4
A_HEADS = 4
A_DQK = 256
A_DV = 512
B_HEADS = 16
B_DK = 128
B_DV = 128
M_HEADS = 4
M_HD = 512

LANE = 128
SUBLANE = 8
VMEM_LIMIT = 56 * 1024 * 1024
MLSTM_CHUNK = 128
MLSTM_GROUP = 4
HGRN_CHUNK = 64
HGRN_UNROLL = 4
HGRN_SUB = 4
LOG2E = 1.4426950408889634

_NT = (((1,), (1,)), ((), ()))
_TN = (((0,), (0,)), ((), ()))


def _params(*sem):
    return pltpu.CompilerParams(dimension_semantics=sem, vmem_limit_bytes=VMEM_LIMIT)


def _tile(n, pref, align):
    best = None
    t = align
    while t <= min(n, pref):
        if n % t == 0:
            best = t
        t += align
    return best if best is not None else n


def _dot(a, b):
    return jnp.dot(a, b, preferred_element_type=F32)


def _log_sigmoid(x):
    return jnp.minimum(x, 0.0) - jnp.log1p(jnp.exp(-jnp.abs(x)))


def _silu(x):
    return x * jax.nn.sigmoid(x)


def _silu_tanh(x):
    hx = 0.5 * x
    return hx + hx * jnp.tanh(hx)


def _rms(x):
    return x * lax.rsqrt(jnp.mean(x * x, axis=-1, keepdims=True) + EPS)


def _rmsnorm_kernel(x_ref, g_ref, o_ref):
    o_ref[...] = (_rms(x_ref[...]) * g_ref[...]).astype(o_ref.dtype)


def _rmsnorm(x, g, out_dtype):
    m, d = x.shape
    tm = _tile(m, 512, 16)
    return pl.pallas_call(
        _rmsnorm_kernel,
        grid=(m // tm,),
        in_specs=[pl.BlockSpec((tm, d), lambda i: (i, 0)), pl.BlockSpec((1, d), lambda i: (0, 0))],
        out_specs=pl.BlockSpec((tm, d), lambda i: (i, 0)),
        out_shape=jax.ShapeDtypeStruct((m, d), out_dtype),
        compiler_params=_params("parallel"),
        name="rmsnorm",
    )(x, g.reshape(1, d))


def _matmul_kernel(a_ref, w_ref, o_ref):
    o_ref[...] = _dot(a_ref[...], w_ref[...]).astype(o_ref.dtype)


def _matmul(a, w, layer, out_dtype, *, col_off=0, n_cols=None, tm=2048, tn=512):
    m, k = a.shape
    n_cols = w.shape[2] - col_off if n_cols is None else n_cols
    tm = _tile(m, tm, 16)
    tn = _tile(n_cols, tn, LANE)
    assert col_off % tn == 0
    joff = col_off // tn
    return pl.pallas_call(
        _matmul_kernel,
        grid=(m // tm, n_cols // tn),
        in_specs=[pl.BlockSpec((tm, k), lambda i, j: (i, 0)),
                  pl.BlockSpec((None, k, tn), lambda i, j: (layer, 0, j + joff))],
        out_specs=pl.BlockSpec((tm, tn), lambda i, j: (i, j)),
        out_shape=jax.ShapeDtypeStruct((m, n_cols), out_dtype),
        compiler_params=_params("parallel", "arbitrary"),
        name="matmul",
    )(a, w)


def _matmul_side_kernel(a_ref, w_ref, ws_ref, o_ref, os_ref):
    a = a_ref[...]
    o_ref[...] = _dot(a, w_ref[...])

    @pl.when(pl.program_id(1) == 0)
    def _():
        os_ref[...] = _dot(a, ws_ref[...])


def _matmul_side(a, w, ws, layer, *, tm=2048, tn=512):
    m, k = a.shape
    n, ns = w.shape[2], ws.shape[2]
    tm = _tile(m, tm, 16)
    tn = _tile(n, tn, LANE)
    return pl.pallas_call(
        _matmul_side_kernel,
        grid=(m // tm, n // tn),
        in_specs=[pl.BlockSpec((tm, k), lambda i, j: (i, 0)),
                  pl.BlockSpec((None, k, tn), lambda i, j: (layer, 0, j)),
                  pl.BlockSpec((None, k, ns), lambda i, j: (layer, 0, 0))],
        out_specs=[pl.BlockSpec((tm, tn), lambda i, j: (i, j)),
                   pl.BlockSpec((tm, ns), lambda i, j: (i, 0))],
        out_shape=[jax.ShapeDtypeStruct((m, n), F32), jax.ShapeDtypeStruct((m, ns), F32)],
        compiler_params=_params("parallel", "arbitrary"),
        name="matmul_side",
    )(a, w, ws)


def _mm_res_norm_kernel(a_ref, w_ref, x_ref, g_ref, xo_ref, ho_ref):
    x = x_ref[...] + _dot(a_ref[...], w_ref[...])
    xo_ref[...] = x
    ho_ref[...] = (_rms(x) * g_ref[...]).astype(ho_ref.dtype)


def _mm_res_norm(a, w, layer, x, g, norm_dtype, *, tm):
    m, k = a.shape
    d = w.shape[2]
    tm = _tile(m, tm, 16)
    return pl.pallas_call(
        _mm_res_norm_kernel,
        grid=(m // tm,),
        in_specs=[pl.BlockSpec((tm, k), lambda i: (i, 0)),
                  pl.BlockSpec((None, k, d), lambda i: (layer, 0, 0), pipeline_mode=pl.Buffered(1)),
                  pl.BlockSpec((tm, d), lambda i: (i, 0)),
                  pl.BlockSpec((1, d), lambda i: (0, 0))],
        out_specs=[pl.BlockSpec((tm, d), lambda i: (i, 0)),
                   pl.BlockSpec((tm, d), lambda i: (i, 0))],
        out_shape=[jax.ShapeDtypeStruct((m, d), F32), jax.ShapeDtypeStruct((m, d), norm_dtype)],
        compiler_params=_params("parallel"),
        name="mm_res_norm",
    )(a, w, x, g.reshape(1, d))


def _ffn_in_kernel(h_ref, wg_ref, wu_ref, o_ref):
    h = h_ref[...]
    g = _dot(h, wg_ref[...])
    u = _dot(h, wu_ref[...])
    o_ref[...] = (_silu_tanh(g) * u).astype(o_ref.dtype)


def _ffn_in(h, w, layer, *, tm=1024, tn=512):
    m, d = h.shape
    dff = w.shape[2] // 2
    tm = _tile(m, tm, 16)
    tn = _tile(dff, tn, LANE)
    nj = dff // tn
    return pl.pallas_call(
        _ffn_in_kernel,
        grid=(m // tm, nj),
        in_specs=[pl.BlockSpec((tm, d), lambda i, j: (i, 0)),
                  pl.BlockSpec((None, d, tn), lambda i, j: (layer, 0, j)),
                  pl.BlockSpec((None, d, tn), lambda i, j: (layer, 0, j + nj))],
        out_specs=pl.BlockSpec((tm, tn), lambda i, j: (i, j)),
        out_shape=jax.ShapeDtypeStruct((m, dff), BF16),
        compiler_params=_params("parallel", "arbitrary"),
        name="ffn_in",
    )(h, w, w)


def _mix_kernel(ya_ref, yb_ref, ym_ref, wa_ref, wb_ref, wm_ref, ga_ref, gb_ref, gm_ref, o_ref):
    def gate(ref):
        return 0.5 + 0.5 * jnp.tanh(0.5 * ref[...].astype(F32))

    acc = gate(ga_ref) * _dot(ya_ref[...], wa_ref[...])
    acc = acc + gate(gb_ref) * _dot(yb_ref[...], wb_ref[...])
    acc = acc + gate(gm_ref) * _dot(ym_ref[...], wm_ref[...])
    o_ref[...] = acc.astype(o_ref.dtype)


def _mix(ya, yb, ym, wa, wb, wm, layer, p_f, gate_off, d, *, tm=1024, tn=512):
    m = ya.shape[0]
    tm = _tile(m, tm, 16)
    tn = _tile(d, tn, LANE)
    assert gate_off % tn == 0
    goff = gate_off // tn
    nj = d // tn

    def y_spec(y):
        return pl.BlockSpec((tm, y.shape[1]), lambda i, j: (i, 0))

    def w_spec(w):
        return pl.BlockSpec((None, w.shape[1], tn), lambda i, j: (layer, 0, j))

    def g_spec(which):
        return pl.BlockSpec((tm, tn), lambda i, j: (i, goff + which * nj + j))

    return pl.pallas_call(
        _mix_kernel,
        grid=(m // tm, nj),
        in_specs=[y_spec(ya), y_spec(yb), y_spec(ym), w_spec(wa), w_spec(wb), w_spec(wm),
                  g_spec(0), g_spec(1), g_spec(2)],
        out_specs=pl.BlockSpec((tm, tn), lambda i, j: (i, j)),
        out_shape=jax.ShapeDtypeStruct((m, d), BF16),
        compiler_params=_params("parallel", "arbitrary"),
        name="mix",
    )(ya, yb, ym, wa, wb, wm, p_f, p_f, p_f)


def _memattn_kernel(q_ref, k_ref, v_ref, o_ref, k16, v16, *, heads, hd):
    @pl.when(pl.program_id(1) == 0)
    def _():
        for h in range(heads):
            if len(k_ref.shape) == 2:
                k16[h] = k_ref[:, h * hd:(h + 1) * hd].astype(BF16)
                v16[h] = v_ref[:, h * hd:(h + 1) * hd].astype(BF16)
            else:
                k16[h] = k_ref[:, h, :].astype(BF16)
                v16[h] = v_ref[:, h, :].astype(BF16)

    for h in range(heads):
        sl = slice(h * hd, (h + 1) * hd)
        s = lax.dot_general(q_ref[:, sl], k16[h], _NT, preferred_element_type=F32) * (hd ** -0.5)
        p = jnp.exp(s - jnp.max(s, axis=-1, keepdims=True))
        p = p / jnp.sum(p, axis=-1, keepdims=True)
        o_ref[:, sl] = _dot(p.astype(BF16), v16[h]).astype(o_ref.dtype)


def _memattn(p_bf, q_off, memk, memv, layer, batch, t, *, tq=512):
    n_mem = memk.shape[2]
    heads, hd = M_HEADS, M_HD
    w = heads * hd
    tq = _tile(t, tq, 16)
    nt = t // tq
    assert q_off % w == 0
    qoff = q_off // w
    mem_block = (None, None) + memk.shape[2:]
    mem_spec = pl.BlockSpec(mem_block, lambda b, i: (layer, b) + (0,) * (len(mem_block) - 2))
    return pl.pallas_call(
        functools.partial(_memattn_kernel, heads=heads, hd=hd),
        grid=(batch, nt),
        in_specs=[pl.BlockSpec((tq, w), lambda b, i: (b * nt + i, qoff)), mem_spec, mem_spec],
        out_specs=pl.BlockSpec((tq, w), lambda b, i: (b * nt + i, 0)),
        out_shape=jax.ShapeDtypeStruct((batch * t, w), BF16),
        scratch_shapes=[pltpu.VMEM((heads, n_mem, hd), BF16), pltpu.VMEM((heads, n_mem, hd), BF16)],
        compiler_params=_params("parallel", "arbitrary"),
        name="memattn",
    )(p_bf, memk, memv)


def _split3(x):
    hi = x.astype(BF16)
    r = x - hi.astype(F32)
    mid = r.astype(BF16)
    lo = (r - mid.astype(F32)).astype(BF16)
    return hi, mid, lo


def _mlstm_kernel(*refs, L, nc, heads, dqk, dv, has_state, has_prev):
    if has_prev:
        refs = refs[1:]
    if has_state:
        (qk_ref, v_ref, og_ref, ifc_ref, ifr_ref, bc_ref, br_ref, cw_ref, cb_ref, g_ref,
         c0_ref, n0_ref, m0_ref, conv0_ref,
         y_ref, c_out, n_out, m_out, conv_out, ubuf, caug, m_scr, corr_scr) = refs
    else:
        (qk_ref, v_ref, og_ref, ifc_ref, ifr_ref, bc_ref, br_ref, cw_ref, cb_ref, g_ref,
         y_ref, c_out, n_out, m_out, conv_out, ubuf, caug, m_scr, corr_scr) = refs
    c_idx = pl.program_id(1)
    nq = heads * dqk

    @pl.when(c_idx == 0)
    def _init():
        if has_state:
            ubuf[...] = conv0_ref[...]
            caug[:, :, :dv] = c0_ref[...]
            caug[:, :, dv:] = n0_ref[...]
            m_scr[...] = m0_ref[...]
        else:
            ubuf[...] = jnp.zeros((SUBLANE, 2 * nq), F32)
            caug[...] = jnp.zeros(caug.shape, F32)
            m_scr[...] = jnp.zeros(m_scr.shape, F32)

    pre_c = ifc_ref[...] + bc_ref[...]
    pre_r = ifr_ref[...] + br_ref[...]
    lf_c = _log_sigmoid(pre_c)
    lf_r = _log_sigmoid(pre_r)
    row = lax.broadcasted_iota(jnp.int32, (L, L), 0)
    col = lax.broadcasted_iota(jnp.int32, (L, L), 1)
    causal = row >= col
    tri = jnp.where(causal, 1.0, 0.0).astype(BF16)
    tri_t = jnp.where(row <= col, 1.0, 0.0).astype(BF16)
    fcum_c = sum(_dot(tri, part) for part in _split3(lf_c))
    fcum_r = sum(_dot(part, tri_t) for part in _split3(lf_r))

    shift = jnp.concatenate(
        [jnp.where(row - col == CONV_W - 1 - j, 1.0, 0.0).astype(BF16) for j in range(CONV_W - 1)], axis=0)
    sub = lax.broadcasted_iota(jnp.int32, (SUBLANE, 2 * nq), 0)
    corr = jnp.zeros((SUBLANE, 2 * nq), F32)
    for t0 in range(CONV_W - 1):
        r = sum(cw_ref[j:j + 1, :] * ubuf[SUBLANE - (CONV_W - 1) + t0 + j:SUBLANE - (CONV_W - 2) + t0 + j, :]
                for j in range(CONV_W - 1 - t0))
        corr = jnp.where(sub == t0, r, corr)
    corr_scr[...] = corr

    ones_blk = jnp.where(lax.broadcasted_iota(jnp.int32, (L, LANE), 1) == 0, 1.0, 0.0).astype(BF16)

    def conv(c0, scale):
        sh = _dot(shift, qk_ref[:, c0:c0 + dqk])
        blocks = []
        for b0 in range(0, dqk, LANE):
            cs = slice(c0 + b0, c0 + b0 + LANE)
            acc = cb_ref[:, cs] + cw_ref[CONV_W - 1:CONV_W, cs] * qk_ref[:, cs].astype(F32)
            for j in range(CONV_W - 1):
                acc = acc + cw_ref[j:j + 1, cs] * sh[j * L:(j + 1) * L, b0:b0 + LANE]
            acc = jnp.concatenate([acc[:SUBLANE] + corr_scr[:, cs], acc[SUBLANE:]], axis=0)
            act = _silu_tanh(acc)
            blocks.append((act if scale is None else act * scale).astype(BF16))
        return jnp.concatenate(blocks, axis=1)

    def head_group(hs):
        q16 = {h: conv(h * dqk, None) for h in hs}
        k16 = {h: conv(nq + h * dqk, dqk ** -0.5) for h in hs}
        qc = {h: _dot(q16[h], caug[h].astype(BF16)) for h in hs}
        s_raw = {h: lax.dot_general(q16[h], k16[h], _NT, preferred_element_type=F32) for h in hs}

        m_t, w_state, w_end, decay, s16 = {}, {}, {}, {}, {}
        for h in hs:
            fc_c = fcum_c[:, heads + h:heads + h + 1]
            fc_r = fcum_r[heads + h:heads + h + 1, :]
            li_c = pre_c[:, h:h + 1]
            li_r = pre_r[h:h + 1, :]
            m_prev = m_scr[h, 0:1, 0:1]
            d = jnp.where(causal, fc_c - fc_r + li_r, NEG_BIG)
            inter = fc_c + m_prev
            m_t[h] = jnp.maximum(inter, jnp.max(d, axis=1, keepdims=True))
            w_state[h] = jnp.exp(inter - m_t[h])
            s16[h] = (s_raw[h] * jnp.exp(d - m_t[h])).astype(BF16)
            m_new = m_t[h][L - 1:L, :]
            fc_last = fc_c[L - 1:L, :]
            decay[h] = jnp.exp(fc_last + m_prev - m_new)
            w_end[h] = jnp.exp(fc_last - fc_c + li_c - m_new)
            m_scr[h] = jnp.broadcast_to(m_new, (SUBLANE, LANE))

        def v_of(h):
            return v_ref[:, h * dv:(h + 1) * dv]

        num = {h: _dot(s16[h], v_of(h)) for h in hs}
        dsum = {h: _dot(s16[h], ones_blk) for h in hs}
        kw = {h: k16[h] * w_end[h].astype(BF16) for h in hs}

        for h in hs:
            den = dsum[h][:, 0:1] + w_state[h] * qc[h][:, dv:dv + 1]
            rden = 1.0 / jnp.maximum(jnp.abs(den), jnp.exp(-m_t[h]))
            hb, ssq = [], 0.0
            for b0 in range(0, dv, LANE):
                blk = (num[h][:, b0:b0 + LANE] + w_state[h] * qc[h][:, b0:b0 + LANE]) * rden
                ssq = ssq + jnp.sum(blk * blk, axis=1, keepdims=True)
                hb.append(blk)
            rinv = lax.rsqrt(ssq * (1.0 / dv) + EPS)
            for i, b0 in enumerate(range(0, dv, LANE)):
                vs = slice(h * dv + b0, h * dv + b0 + LANE)
                gate = 0.5 + 0.5 * jnp.tanh(0.5 * og_ref[:, vs].astype(F32))
                y_ref[:, vs] = (hb[i] * rinv * g_ref[:, vs] * gate).astype(y_ref.dtype)

        for h in hs:
            caug[h, :, :dv] = decay[h] * caug[h, :, :dv] + lax.dot_general(
                kw[h], v_of(h), _TN, preferred_element_type=F32)
            caug[h, :, dv:] = decay[h] * caug[h, :, dv:] + lax.dot_general(
                kw[h], ones_blk, _TN, preferred_element_type=F32)

    for h0 in range(0, heads, MLSTM_GROUP):
        head_group(list(range(h0, min(h0 + MLSTM_GROUP, heads))))

    ubuf[...] = qk_ref[L - 2 * SUBLANE:L, :].astype(F32)[SUBLANE:, :]

    @pl.when(c_idx == nc - 1)
    def _fin():
        c_out[...] = caug[:, :, :dv]
        n_out[...] = caug[:, :, dv:]
        m_out[...] = m_scr[...]
        conv_out[...] = ubuf[...]


def _mlstm(p_f, qk_off, og_off, p_bf, v_off, p_if, b_i, b_f, conv_w, conv_b, g, state, c_acc, layer, batch, t):
    heads, dqk, dv = A_HEADS, A_DQK, A_DV
    nq, nv = heads * dqk, heads * dv
    m_rows = batch * t
    L = min(MLSTM_CHUNK, t)
    nc = t // L
    assert t % L == 0 and L >= 2 * SUBLANE and qk_off % (2 * nq) == 0 and og_off % nv == 0 and v_off % nv == 0
    has_state = state is not None

    ifr = p_if[:, :SUBLANE].reshape(m_rows // L, L, SUBLANE).transpose(0, 2, 1)
    bias = jnp.concatenate([b_i, b_f]).astype(F32)
    bias_c = jnp.zeros((1, LANE), F32).at[0, :2 * heads].set(bias)
    bias_r = jnp.zeros((SUBLANE, 1), F32).at[:2 * heads, 0].set(bias)

    def rows(width, off):
        return pl.BlockSpec((L, width), lambda b, c: (b * nc + c, off // width))

    def whole(shape):
        return pl.BlockSpec(shape, lambda b, c: (0,) * len(shape))

    in_specs = [rows(2 * nq, qk_off), rows(nv, v_off), rows(nv, og_off), rows(LANE, 0),
                pl.BlockSpec((None, SUBLANE, L), lambda b, c: (b * nc + c, 0, 0)),
                whole((1, LANE)), whole((SUBLANE, 1)), whole((CONV_W, 2 * nq)), whole((1, 2 * nq)),
                whole((1, nv))]
    args = [p_f, p_bf, p_f, p_if, ifr, bias_c, bias_r, conv_w, conv_b.reshape(1, 2 * nq), g.reshape(1, nv)]
    if has_state:
        c0, n0, m0, conv0 = state
        n0p = jnp.pad(n0[layer][..., None], ((0, 0), (0, 0), (0, 0), (0, LANE - 1)))
        m0p = jnp.broadcast_to(m0[layer][:, :, None, None], (batch, heads, SUBLANE, LANE))
        conv0p = jnp.pad(conv0[layer], ((0, 0), (SUBLANE - (CONV_W - 1), 0), (0, 0)))
        in_specs += [pl.BlockSpec((None, None, heads, dqk, dv), lambda b, c: (layer, b, 0, 0, 0)),
                     pl.BlockSpec((None, heads, dqk, LANE), lambda b, c: (b, 0, 0, 0)),
                     pl.BlockSpec((None, heads, SUBLANE, LANE), lambda b, c: (b, 0, 0, 0)),
                     pl.BlockSpec((None, SUBLANE, 2 * nq), lambda b, c: (b, 0, 0))]
        args += [c0, n0p, m0p, conv0p]

    aliases = {}
    if c_acc is not None:
        in_specs = [pl.BlockSpec(memory_space=pl.ANY)] + in_specs
        args = [c_acc] + args
        aliases = {0: 1}
    out_specs = [pl.BlockSpec((L, nv), lambda b, c: (b * nc + c, 0)),
                 pl.BlockSpec((None, None, heads, dqk, dv), lambda b, c: (layer, b, 0, 0, 0)),
                 pl.BlockSpec((None, heads, dqk, LANE), lambda b, c: (b, 0, 0, 0)),
                 pl.BlockSpec((None, heads, SUBLANE, LANE), lambda b, c: (b, 0, 0, 0)),
                 pl.BlockSpec((None, SUBLANE, 2 * nq), lambda b, c: (b, 0, 0))]
    out_shape = [jax.ShapeDtypeStruct((m_rows, nv), BF16),
                 jax.ShapeDtypeStruct((DEPTH, batch, heads, dqk, dv), F32),
                 jax.ShapeDtypeStruct((batch, heads, dqk, LANE), F32),
                 jax.ShapeDtypeStruct((batch, heads, SUBLANE, LANE), F32),
                 jax.ShapeDtypeStruct((batch, SUBLANE, 2 * nq), F32)]
    ya, c_new, n_new, m_new, conv_new = pl.pallas_call(
        functools.partial(_mlstm_kernel, L=L, nc=nc, heads=heads, dqk=dqk, dv=dv, has_state=has_state,
                          has_prev=c_acc is not None),
        grid=(batch, nc),
        in_specs=in_specs,
        out_specs=out_specs,
        out_shape=out_shape,
        input_output_aliases=aliases,
        scratch_shapes=[pltpu.VMEM((SUBLANE, 2 * nq), F32),
                        pltpu.VMEM((heads, dqk, dv + LANE), F32),
                        pltpu.VMEM((heads, SUBLANE, LANE), F32),
                        pltpu.VMEM((SUBLANE, 2 * nq), F32)],
        compiler_params=_params("parallel", "arbitrary"),
        name="mlstm",
    )(*args)
    return ya, c_new, n_new[..., 0], m_new[:, :, 0, 0], conv_new[:, SUBLANE - (CONV_W - 1):, :]


def _neg_abs(x):
    bits = lax.bitcast_convert_type(x, jnp.uint32) | jnp.uint32(0x80000000)
    return lax.bitcast_convert_type(bits, F32)


def _ref_rows(g_scr, L, bs, n):
    hs = bs // 2

    def bcast(r, rows):
        return jnp.broadcast_to(g_scr[r:r + 1, :], (rows, n))

    if bs >= 2 * SUBLANE:
        return jnp.concatenate([bcast(b0 + hs - 1, bs) for b0 in range(0, L, bs)], axis=0)
    sub = lax.broadcasted_iota(jnp.int32, (SUBLANE, n), 0)
    groups = []
    for g0 in range(0, L, SUBLANE):
        r0 = g0 + hs - 1
        val = bcast(r0, SUBLANE)
        for b0 in range(bs, SUBLANE, bs):
            val = jnp.where(sub >= b0, bcast(r0 + b0, SUBLANE), val)
        groups.append(val)
    return jnp.concatenate(groups, axis=0)


def _hgrn_kernel(*refs, L, nsub, nc, heads, dk, dv, layer, has_state, has_prev):
    if has_prev:
        refs = refs[1:]
    if has_state:
        q_ref, f_ref, v_ref, gt_ref, lbraw_ref, g_ref, s0_ref, y_ref, s_out, st, lb_scr, g_scr = refs
    else:
        q_ref, f_ref, v_ref, gt_ref, lbraw_ref, g_ref, y_ref, s_out, st, lb_scr, g_scr = refs
    c_idx = pl.program_id(1)

    @pl.when(c_idx == 0)
    def _init():
        for h in range(heads):
            st[h] = s0_ref[h].T if has_state else jnp.zeros((dv, dk), F32)
        raw = lbraw_ref[...]
        e = jnp.exp(raw - jnp.max(raw, axis=0, keepdims=True))
        sm = e / jnp.sum(e, axis=0, keepdims=True)
        lb = jnp.zeros((1, heads * dk), F32)
        for j in range(1, layer + 1):
            lb = lb + sm[j:j + 1, :]
        lb_scr[...] = lb

    row = lax.broadcasted_iota(jnp.int32, (L, L), 0)
    col = lax.broadcasted_iota(jnp.int32, (L, L), 1)
    tri = jnp.where(row >= col, 1.0, 0.0).astype(BF16)
    diag = row == col
    levels = []
    bs = 2
    while bs <= L:
        hs, lg = bs // 2, bs.bit_length() - 1
        pair = ((row >> lg) == (col >> lg)) & ((row & (bs - 1)) >= hs) & ((col & (bs - 1)) < hs)
        levels.append((bs, pair))
        bs *= 2

    def nt(a, b):
        return lax.dot_general(a, b, _NT, preferred_element_type=F32)

    def head_group(hs, sub):
        rs = pl.ds(sub * L, L)
        kb16, qb16, gs, f16 = [], [], [], []
        for h in hs:
            kc = pl.ds(h * dk, dk)
            z = f_ref[rs, kc]
            lb = lb_scr[:, kc]
            t = jnp.exp(_neg_abs(z))
            r = 1.0 / (1.0 + t)
            pos = z >= 0.0
            f = jnp.where(pos, 1.0 + lb * t, lb + t) * r
            kb = (1.0 - lb) * (jnp.where(pos, t, 1.0) * r)
            qb = _silu_tanh(q_ref[rs, kc].astype(F32))
            kb16.append(kb.astype(BF16))
            qb16.append(qb.astype(BF16))
            f16.append(f.astype(BF16))
            gs.append(_dot(tri, jnp.concatenate(_split3(jnp.log(f) * LOG2E)[:2], axis=1)))
        g2, pend = [], []
        for u, h in enumerate(hs):
            g2.append(gs[u][:, :dk] + gs[u][:, dk:])
            g_scr[sub * heads + h] = g2[u]
            pend.append(nt(qb16[u], kb16[u]))
        a = [0.0] * len(hs)
        mask = diag
        for bs, pair in levels:
            new = []
            for u, h in enumerate(hs):
                if bs == 2:
                    new.append(nt(qb16[u] * f16[u], kb16[u]))
                    continue
                e = jnp.exp2(_neg_abs(g2[u] - _ref_rows(g_scr.at[sub * heads + h], L, bs, dk))).astype(BF16)
                new.append(nt(qb16[u] * e, kb16[u] * e))
            a = [jnp.where(mask, p, x) for p, x in zip(pend, a)]
            pend, mask = new, pair
        a = [jnp.where(mask, p, x) for p, x in zip(pend, a)]
        o = []
        for u, h in enumerate(hs):
            vc = pl.ds(h * dv, dv)
            qg = qb16[u] * jnp.exp2(g2[u]).astype(BF16)
            o.append((_dot(a[u].astype(BF16), v_ref[rs, vc]), nt(qg, st[h].astype(BF16))))
        upd = []
        for u, h in enumerate(hs):
            g_end = g2[u][L - 1:L, :]
            ke = kb16[u] * jnp.exp2(g_end - g2[u]).astype(BF16)
            upd.append(lax.dot_general(v_ref[rs, pl.ds(h * dv, dv)], ke, _TN, preferred_element_type=F32))
        for u, h in enumerate(hs):
            vc = pl.ds(h * dv, dv)
            y = _rms(o[u][0] + o[u][1]) * g_ref[...] * _silu_tanh(gt_ref[rs, vc].astype(F32))
            y_ref[rs, vc] = y.astype(y_ref.dtype)
        for u, h in enumerate(hs):
            st[h] = jnp.exp2(g2[u][L - 1:L, :]) * st[h] + upd[u]

    for sub in range(nsub):
        for h0 in range(0, heads, HGRN_UNROLL):
            head_group(list(range(h0, h0 + HGRN_UNROLL)), sub)

    @pl.when(c_idx == nc - 1)
    def _fin():
        for h in range(heads):
            s_out[h] = st[h].T


def _hgrn(p_f, q_off, f_off, gt_off, p_bf, v_off, lb_raw, g, s0, s_acc, layer, batch, t):
    heads, dk, dv = B_HEADS, B_DK, B_DV
    nk, nv = heads * dk, heads * dv
    L = min(HGRN_CHUNK, t)
    nsub = HGRN_SUB if t % (L * HGRN_SUB) == 0 else 1
    nc = t // (L * nsub)
    assert t % L == 0 and q_off % nk == 0 and f_off % nk == 0 and gt_off % nv == 0 and v_off % nv == 0
    assert heads % HGRN_UNROLL == 0
    has_state = s0 is not None

    def rows(width, off):
        return pl.BlockSpec((L * nsub, width), lambda b, c: (b * nc + c, off // width))

    in_specs = [rows(nk, q_off), rows(nk, f_off), rows(nv, v_off), rows(nv, gt_off),
                pl.BlockSpec(lb_raw.shape, lambda b, c: (0, 0)),
                pl.BlockSpec((1, dv), lambda b, c: (0, 0))]
    args = [p_bf, p_f, p_bf, p_bf, lb_raw, g.reshape(1, dv)]
    if has_state:
        in_specs.append(pl.BlockSpec((None, None, heads, dk, dv), lambda b, c: (layer, b, 0, 0, 0)))
        args.append(s0)
    aliases = {}
    if s_acc is not None:
        in_specs = [pl.BlockSpec(memory_space=pl.ANY)] + in_specs
        args = [s_acc] + args
        aliases = {0: 1}
    return pl.pallas_call(
        functools.partial(_hgrn_kernel, L=L, nsub=nsub, nc=nc, heads=heads, dk=dk, dv=dv, layer=layer,
                          has_state=has_state, has_prev=s_acc is not None),
        grid=(batch, nc),
        in_specs=in_specs,
        out_specs=[pl.BlockSpec((L * nsub, nv), lambda b, c: (b * nc + c, 0)),
                   pl.BlockSpec((None, None, heads, dk, dv), lambda b, c: (layer, b, 0, 0, 0))],
        out_shape=[jax.ShapeDtypeStruct((batch * t, nv), BF16),
                   jax.ShapeDtypeStruct((DEPTH, batch, heads, dk, dv), F32)],
        input_output_aliases=aliases,
        scratch_shapes=[pltpu.VMEM((heads, dv, dk), F32),
                        pltpu.VMEM((1, nk), F32),
                        pltpu.VMEM((nsub * heads, L, dk), F32)],
        compiler_params=_params("parallel", "arbitrary"),
        name="hgrn",
    )(*args)


def _prep_weights(w_in, w_proj_a, w_proj_b, w_proj_m, w_out, w_ffn_in, w_ffn_out, w_mem_kv):
    a_qk = 2 * A_HEADS * A_DQK
    a_v = A_HEADS * A_DV
    b_k = B_HEADS * B_DK
    b_v = B_HEADS * B_DV
    m_w = M_HEADS * M_HD
    sizes = [a_qk, a_v, a_v, 2 * A_HEADS, b_k, b_k, b_v, b_v, m_w]
    starts = [0]
    for s in sizes:
        starts.append(starts[-1] + s)
    qk_a, v_a, o_a, if_a, q_b, f_b, i_b, gt_b, q_m = [
        w_in[:, :, starts[i]:starts[i + 1]] for i in range(len(sizes))]
    gates = w_in[:, :, starts[-1]:]
    w_bf = jnp.concatenate([v_a, i_b, q_m, qk_a, o_a, q_b, gt_b, gates], axis=-1).astype(BF16)
    w_f = f_b.astype(BF16)
    w_if = jnp.pad(if_a, ((0, 0), (0, 0), (0, LANE - 2 * A_HEADS))).astype(BF16)
    offs, pos = {}, 0
    for name, width in (("v_a", a_v), ("i_b", b_v), ("q_m", m_w), ("qk_a", a_qk), ("o_a", a_v),
                        ("q_b", b_k), ("gt_b", b_v), ("gates", 0)):
        offs[name] = pos
        pos += width
    return dict(w_bf=w_bf, w_f=w_f, w_if=w_if, offs=offs,
                w_proj_a=w_proj_a.astype(BF16), w_proj_b=w_proj_b.astype(BF16),
                w_proj_m=w_proj_m.astype(BF16), w_out=w_out.astype(BF16),
                w_ffn_in=w_ffn_in.astype(BF16), w_ffn_out=w_ffn_out.astype(BF16),
                w_mem_kv=w_mem_kv.astype(BF16))


def _trunk(x, memk, memv, state, W, P):
    batch, t, d = x.shape
    offs = W["offs"]
    x2 = x.reshape(batch * t, d)
    h = _rmsnorm(x2, P["g_mix"][0], BF16)
    new_conv, new_n, new_m = [], [], []
    ca = sb = None
    for l in range(DEPTH):
        p_bf = _matmul(h, W["w_bf"], l, BF16, tn=1024)
        p_f, p_if = _matmul_side(h, W["w_f"], W["w_if"], l)

        ya, ca, na, ma, cbuf = _mlstm(
            p_bf, offs["qk_a"], offs["o_a"], p_bf, offs["v_a"], p_if, P["b_igate"][l], P["b_fgate"][l],
            P["conv_w"][l], P["conv_b"][l], P["g_mlstm"][l],
            None if state is None else state[:4], ca, l, batch, t)
        yb, sb = _hgrn(p_f, offs["q_b"], 0, offs["gt_b"], p_bf, offs["i_b"], P["lb_raw"],
                       P["g_hgrn"][l], None if state is None else state[4], sb, l, batch, t)
        ym = _memattn(p_bf, offs["q_m"], memk, memv, l, batch, t)

        mix = _mix(ya, yb, ym, W["w_proj_a"], W["w_proj_b"], W["w_proj_m"], l, p_bf, offs["gates"], d)
        x2, h2 = _mm_res_norm(mix, W["w_out"], l, x2, P["g_ffn"][l], BF16, tm=512)
        act = _ffn_in(h2, W["w_ffn_in"], l)
        last = l == DEPTH - 1
        x2, h = _mm_res_norm(act, W["w_ffn_out"], l, x2, P["g_final"] if last else P["g_mix"][l + 1],
                             F32 if last else BF16, tm=256)
        new_conv.append(cbuf)
        new_n.append(na)
        new_m.append(ma)
    return (h.reshape(batch, t, d), jnp.stack(new_conv), ca, jnp.stack(new_n), jnp.stack(new_m), sb)


def kernel(x_prompt, x_sample, mem_prompt, cache_mem_k, cache_mem_v, state_mlstm_c, state_mlstm_n, state_mlstm_m, state_mlstm_conv, state_hgrn_s, g_mix, w_in, conv_w, conv_b, b_igate, b_fgate, g_mlstm, lb_raw, g_hgrn, g_mem, w_mem_kv, w_proj_a, w_proj_b, w_proj_m, w_out, g_ffn, w_ffn_in, w_ffn_out, g_final):
    W = _prep_weights(w_in, w_proj_a, w_proj_b, w_proj_m, w_out, w_ffn_in, w_ffn_out, w_mem_kv)
    P = dict(g_mix=g_mix, conv_w=conv_w, conv_b=conv_b, b_igate=b_igate, b_fgate=b_fgate, g_mlstm=g_mlstm,
             lb_raw=lb_raw, g_hgrn=g_hgrn, g_ffn=g_ffn, g_final=g_final)
    batch, n_mem, d = mem_prompt.shape
    dec_batch = x_sample.shape[0]
    m_w = M_HEADS * M_HD

    mem2 = mem_prompt.reshape(batch * n_mem, d)
    ks, vs = [], []
    for l in range(DEPTH):
        mem_n = _rmsnorm(mem2, g_mem[l], BF16)
        ks.append(_matmul(mem_n, W["w_mem_kv"], l, F32, col_off=0, n_cols=m_w))
        vs.append(_matmul(mem_n, W["w_mem_kv"], l, F32, col_off=m_w, n_cols=m_w))
    mem_shape = (DEPTH, batch, n_mem, M_HEADS, M_HD)
    mem_k = jnp.stack(ks).reshape(DEPTH, batch, n_mem, m_w)
    mem_v = jnp.stack(vs).reshape(DEPTH, batch, n_mem, m_w)

    y_p, conv_p, c_p, n_p, m_p, s_p = _trunk(x_prompt, mem_k, mem_v, None, W, P)
    mem_k, mem_v = mem_k.reshape(mem_shape), mem_v.reshape(mem_shape)

    state = (state_mlstm_c, state_mlstm_n, state_mlstm_m, state_mlstm_conv, state_hgrn_s)
    y_s, conv_s, c_s, n_s, m_s, s_s = _trunk(x_sample, cache_mem_k, cache_mem_v, state, W, P)

    return (y_p, y_s, mem_k, mem_v, conv_p, c_p, n_p, m_p, s_p, conv_s, c_s, n_s, m_s, s_s)
```

```python
import functools

import jax
import jax.numpy as jnp
from jax import lax
from jax.experimental import pallas as pl
from jax.experimental.pallas import tpu as pltpu

F32 = jnp.float32
BF16 = jnp.bfloat16

EPS = 1e-6
NEG_BIG = -1e30
DEPTH = 4
CONV_W = 4
A_HEADS = 4
A_DQK = 256
A_DV = 512
B_HEADS = 16
B_DK = 128
B_DV = 128
M_HEADS = 4
M_HD = 512

LANE = 128
SUBLANE = 8
VMEM_LIMIT = 56 * 1024 * 1024
MLSTM_CHUNK = 128
MLSTM_GROUP = 4
HGRN_CHUNK = 64
HGRN_UNROLL = 4
HGRN_SUB = 4
LOG2E = 1.4426950408889634

_NT = (((1,), (1,)), ((), ()))
_TN = (((0,), (0,)), ((), ()))


def _params(*sem):
    return pltpu.CompilerParams(dimension_semantics=sem, vmem_limit_bytes=VMEM_LIMIT)


def _tile(n, pref, align):
    best = None
    t = align
    while t <= min(n, pref):
        if n % t == 0:
            best = t
        t += align
    return best if best is not None else n


def _dot(a, b):
    return jnp.dot(a, b, preferred_element_type=F32)


def _log_sigmoid(x):
    return jnp.minimum(x, 0.0) - jnp.log1p(jnp.exp(-jnp.abs(x)))


def _silu(x):
    return x * jax.nn.sigmoid(x)


def _silu_tanh(x):
    hx = 0.5 * x
    return hx + hx * jnp.tanh(hx)


def _rms(x):
    return x * lax.rsqrt(jnp.mean(x * x, axis=-1, keepdims=True) + EPS)


def _rmsnorm_kernel(x_ref, g_ref, o_ref):
    o_ref[...] = (_rms(x_ref[...]) * g_ref[...]).astype(o_ref.dtype)


def _rmsnorm(x, g, out_dtype):
    m, d = x.shape
    tm = _tile(m, 512, 16)
    return pl.pallas_call(
        _rmsnorm_kernel,
        grid=(m // tm,),
        in_specs=[pl.BlockSpec((tm, d), lambda i: (i, 0)), pl.BlockSpec((1, d), lambda i: (0, 0))],
        out_specs=pl.BlockSpec((tm, d), lambda i: (i, 0)),
        out_shape=jax.ShapeDtypeStruct((m, d), out_dtype),
        compiler_params=_params("parallel"),
        name="rmsnorm",
    )(x, g.reshape(1, d))


def _matmul_kernel(a_ref, w_ref, o_ref):
    o_ref[...] = _dot(a_ref[...], w_ref[...]).astype(o_ref.dtype)


def _matmul(a, w, layer, out_dtype, *, col_off=0, n_cols=None, tm=2048, tn=512):
    m, k = a.shape
    n_cols = w.shape[2] - col_off if n_cols is None else n_cols
    tm = _tile(m, tm, 16)
    tn = _tile(n_cols, tn, LANE)
    assert col_off % tn == 0
    joff = col_off // tn
    return pl.pallas_call(
        _matmul_kernel,
        grid=(m // tm, n_cols // tn),
        in_specs=[pl.BlockSpec((tm, k), lambda i, j: (i, 0)),
                  pl.BlockSpec((None, k, tn), lambda i, j: (layer, 0, j + joff))],
        out_specs=pl.BlockSpec((tm, tn), lambda i, j: (i, j)),
        out_shape=jax.ShapeDtypeStruct((m, n_cols), out_dtype),
        compiler_params=_params("parallel", "arbitrary"),
        name="matmul",
    )(a, w)


def _matmul_side_kernel(a_ref, w_ref, ws_ref, o_ref, os_ref):
    a = a_ref[...]
    o_ref[...] = _dot(a, w_ref[...])

    @pl.when(pl.program_id(1) == 0)
    def _():
        os_ref[...] = _dot(a, ws_ref[...])


def _matmul_side(a, w, ws, layer, *, tm=2048, tn=512):
    m, k = a.shape
    n, ns = w.shape[2], ws.shape[2]
    tm = _tile(m, tm, 16)
    tn = _tile(n, tn, LANE)
    return pl.pallas_call(
        _matmul_side_kernel,
        grid=(m // tm, n // tn),
        in_specs=[pl.BlockSpec((tm, k), lambda i, j: (i, 0)),
                  pl.BlockSpec((None, k, tn), lambda i, j: (layer, 0, j)),
                  pl.BlockSpec((None, k, ns), lambda i, j: (layer, 0, 0))],
        out_specs=[pl.BlockSpec((tm, tn), lambda i, j: (i, j)),
                   pl.BlockSpec((tm, ns), lambda i, j: (i, 0))],
        out_shape=[jax.ShapeDtypeStruct((m, n), F32), jax.ShapeDtypeStruct((m, ns), F32)],
        compiler_params=_params("parallel", "arbitrary"),
        name="matmul_side",
    )(a, w, ws)


def _mm_res_norm_kernel(a_ref, w_ref, x_ref, g_ref, xo_ref, ho_ref):
    x = x_ref[...] + _dot(a_ref[...], w_ref[...])
    xo_ref[...] = x
    ho_ref[...] = (_rms(x) * g_ref[...]).astype(ho_ref.dtype)


def _mm_res_norm(a, w, layer, x, g, norm_dtype, *, tm):
    m, k = a.shape
    d = w.shape[2]
    tm = _tile(m, tm, 16)
    return pl.pallas_call(
        _mm_res_norm_kernel,
        grid=(m // tm,),
        in_specs=[pl.BlockSpec((tm, k), lambda i: (i, 0)),
                  pl.BlockSpec((None, k, d), lambda i: (layer, 0, 0), pipeline_mode=pl.Buffered(1)),
                  pl.BlockSpec((tm, d), lambda i: (i, 0)),
                  pl.BlockSpec((1, d), lambda i: (0, 0))],
        out_specs=[pl.BlockSpec((tm, d), lambda i: (i, 0)),
                   pl.BlockSpec((tm, d), lambda i: (i, 0))],
        out_shape=[jax.ShapeDtypeStruct((m, d), F32), jax.ShapeDtypeStruct((m, d), norm_dtype)],
        compiler_params=_params("parallel"),
        name="mm_res_norm",
    )(a, w, x, g.reshape(1, d))


def _ffn_in_kernel(h_ref, wg_ref, wu_ref, o_ref):
    h = h_ref[...]
    g = _dot(h, wg_ref[...])
    u = _dot(h, wu_ref[...])
    o_ref[...] = (_silu_tanh(g) * u).astype(o_ref.dtype)


def _ffn_in(h, w, layer, *, tm=1024, tn=512):
    m, d = h.shape
    dff = w.shape[2] // 2
    tm = _tile(m, tm, 16)
    tn = _tile(dff, tn, LANE)
    nj = dff // tn
    return pl.pallas_call(
        _ffn_in_kernel,
        grid=(m // tm, nj),
        in_specs=[pl.BlockSpec((tm, d), lambda i, j: (i, 0)),
                  pl.BlockSpec((None, d, tn), lambda i, j: (layer, 0, j)),
                  pl.BlockSpec((None, d, tn), lambda i, j: (layer, 0, j + nj))],
        out_specs=pl.BlockSpec((tm, tn), lambda i, j: (i, j)),
        out_shape=jax.ShapeDtypeStruct((m, dff), BF16),
        compiler_params=_params("parallel", "arbitrary"),
        name="ffn_in",
    )(h, w, w)


def _mix_kernel(ya_ref, yb_ref, ym_ref, wa_ref, wb_ref, wm_ref, ga_ref, gb_ref, gm_ref, o_ref):
    def gate(ref):
        return 0.5 + 0.5 * jnp.tanh(0.5 * ref[...].astype(F32))

    acc = gate(ga_ref) * _dot(ya_ref[...], wa_ref[...])
    acc = acc + gate(gb_ref) * _dot(yb_ref[...], wb_ref[...])
    acc = acc + gate(gm_ref) * _dot(ym_ref[...], wm_ref[...])
    o_ref[...] = acc.astype(o_ref.dtype)


def _mix(ya, yb, ym, wa, wb, wm, layer, p_f, gate_off, d, *, tm=1024, tn=512):
    m = ya.shape[0]
    tm = _tile(m, tm, 16)
    tn = _tile(d, tn, LANE)
    assert gate_off % tn == 0
    goff = gate_off // tn
    nj = d // tn

    def y_spec(y):
        return pl.BlockSpec((tm, y.shape[1]), lambda i, j: (i, 0))

    def w_spec(w):
        return pl.BlockSpec((None, w.shape[1], tn), lambda i, j: (layer, 0, j))

    def g_spec(which):
        return pl.BlockSpec((tm, tn), lambda i, j: (i, goff + which * nj + j))

    return pl.pallas_call(
        _mix_kernel,
        grid=(m // tm, nj),
        in_specs=[y_spec(ya), y_spec(yb), y_spec(ym), w_spec(wa), w_spec(wb), w_spec(wm),
                  g_spec(0), g_spec(1), g_spec(2)],
        out_specs=pl.BlockSpec((tm, tn), lambda i, j: (i, j)),
        out_shape=jax.ShapeDtypeStruct((m, d), BF16),
        compiler_params=_params("parallel", "arbitrary"),
        name="mix",
    )(ya, yb, ym, wa, wb, wm, p_f, p_f, p_f)


def _memattn_kernel(q_ref, k_ref, v_ref, o_ref, k16, v16, *dma, heads, hd, layer, nb):
    b = pl.program_id(0)

    if dma:
        kbuf, vbuf, sem = dma

        def copies(bb, slot):
            return [pltpu.make_async_copy(src.at[layer, bb, :, h, :], dst.at[slot, h], sem.at[slot, i * heads + h])
                    for i, (src, dst) in enumerate(((k_ref, kbuf), (v_ref, vbuf))) for h in range(heads)]

    @pl.when(pl.program_id(1) == 0)
    def _():
        if dma:
            @pl.when(b == 0)
            def _():
                for c in copies(0, 0):
                    c.start()

            @pl.when(b + 1 < nb)
            def _():
                for c in copies(b + 1, (b + 1) % 2):
                    c.start()

            for c in copies(b, b % 2):
                c.wait()
        for h in range(heads):
            if dma:
                k16[h] = kbuf[b % 2, h].astype(BF16)
                v16[h] = vbuf[b % 2, h].astype(BF16)
            else:
                k16[h] = k_ref[:, h * hd:(h + 1) * hd].astype(BF16)
                v16[h] = v_ref[:, h * hd:(h + 1) * hd].astype(BF16)

    for h in range(heads):
        sl = slice(h * hd, (h + 1) * hd)
        s = lax.dot_general(q_ref[:, sl], k16[h], _NT, preferred_element_type=F32) * (hd ** -0.5)
        p = jnp.exp(s - jnp.max(s, axis=-1, keepdims=True))
        p = p / jnp.sum(p, axis=-1, keepdims=True)
        o_ref[:, sl] = _dot(p.astype(BF16), v16[h]).astype(o_ref.dtype)


def _memattn(p_bf, q_off, memk, memv, layer, batch, t, *, tq=512):
    n_mem = memk.shape[2]
    heads, hd = M_HEADS, M_HD
    w = heads * hd
    tq = _tile(t, tq, 16)
    nt = t // tq
    assert q_off % w == 0
    qoff = q_off // w
    scratch = [pltpu.VMEM((heads, n_mem, hd), BF16), pltpu.VMEM((heads, n_mem, hd), BF16)]
    if memk.ndim == 5:
        mem_spec = pl.BlockSpec(memory_space=pl.ANY)
        scratch += [pltpu.VMEM((2, heads, n_mem, hd), F32), pltpu.VMEM((2, heads, n_mem, hd), F32),
                    pltpu.SemaphoreType.DMA((2, 2 * heads))]
        sem = ("arbitrary", "arbitrary")
    else:
        mem_spec = pl.BlockSpec((None, None, n_mem, w), lambda b, i: (layer, b, 0, 0))
        sem = ("parallel", "arbitrary")
    return pl.pallas_call(
        functools.partial(_memattn_kernel, heads=heads, hd=hd, layer=layer, nb=batch),
        grid=(batch, nt),
        in_specs=[pl.BlockSpec((tq, w), lambda b, i: (b * nt + i, qoff)), mem_spec, mem_spec],
        out_specs=pl.BlockSpec((tq, w), lambda b, i: (b * nt + i, 0)),
        out_shape=jax.ShapeDtypeStruct((batch * t, w), BF16),
        scratch_shapes=scratch,
        compiler_params=_params(*sem),
        name="memattn",
    )(p_bf, memk, memv)


def _split3(x):
    hi = x.astype(BF16)
    r = x - hi.astype(F32)
    mid = r.astype(BF16)
    lo = (r - mid.astype(F32)).astype(BF16)
    return hi, mid, lo


def _mlstm_kernel(*refs, L, nc, heads, dqk, dv, has_state):
    refs = refs[1:]
    if has_state:
        (qk_ref, v_ref, og_ref, ifc_ref, ifr_ref, bc_ref, br_ref, cw_ref, cb_ref, g_ref,
         c0_ref, n0_ref, m0_ref, conv0_ref,
         y_ref, c_out, n_out, m_out, conv_out, ubuf, caug, m_scr, corr_scr) = refs
    else:
        (qk_ref, v_ref, og_ref, ifc_ref, ifr_ref, bc_ref, br_ref, cw_ref, cb_ref, g_ref,
         y_ref, c_out, n_out, m_out, conv_out, ubuf, caug, m_scr, corr_scr) = refs
    c_idx = pl.program_id(1)
    nq = heads * dqk

    @pl.when(c_idx == 0)
    def _init():
        if has_state:
            ubuf[...] = conv0_ref[...]
            caug[:, :, :dv] = c0_ref[...]
            caug[:, :, dv:] = n0_ref[...]
            m_scr[...] = m0_ref[...]
        else:
            ubuf[...] = jnp.zeros((SUBLANE, 2 * nq), F32)
            caug[...] = jnp.zeros(caug.shape, F32)
            m_scr[...] = jnp.zeros(m_scr.shape, F32)

    pre_c = ifc_ref[...] + bc_ref[...]
    pre_r = ifr_ref[...] + br_ref[...]
    lf_c = _log_sigmoid(pre_c)
    lf_r = _log_sigmoid(pre_r)
    row = lax.broadcasted_iota(jnp.int32, (L, L), 0)
    col = lax.broadcasted_iota(jnp.int32, (L, L), 1)
    causal = row >= col
    tri = jnp.where(causal, 1.0, 0.0).astype(BF16)
    tri_t = jnp.where(row <= col, 1.0, 0.0).astype(BF16)
    fcum_c = sum(_dot(tri, part) for part in _split3(lf_c))
    fcum_r = sum(_dot(part, tri_t) for part in _split3(lf_r))

    shift = jnp.concatenate(
        [jnp.where(row - col == CONV_W - 1 - j, 1.0, 0.0).astype(BF16) for j in range(CONV_W - 1)], axis=0)
    sub = lax.broadcasted_iota(jnp.int32, (SUBLANE, 2 * nq), 0)
    corr = jnp.zeros((SUBLANE, 2 * nq), F32)
    for t0 in range(CONV_W - 1):
        r = sum(cw_ref[j:j + 1, :] * ubuf[SUBLANE - (CONV_W - 1) + t0 + j:SUBLANE - (CONV_W - 2) + t0 + j, :]
                for j in range(CONV_W - 1 - t0))
        corr = jnp.where(sub == t0, r, corr)
    corr_scr[...] = corr

    ones_blk = jnp.where(lax.broadcasted_iota(jnp.int32, (L, LANE), 1) == 0, 1.0, 0.0).astype(BF16)

    def conv(c0, scale):
        sh = _dot(shift, qk_ref[:, c0:c0 + dqk])
        blocks = []
        for b0 in range(0, dqk, LANE):
            cs = slice(c0 + b0, c0 + b0 + LANE)
            acc = cb_ref[:, cs] + cw_ref[CONV_W - 1:CONV_W, cs] * qk_ref[:, cs].astype(F32)
            for j in range(CONV_W - 1):
                acc = acc + cw_ref[j:j + 1, cs] * sh[j * L:(j + 1) * L, b0:b0 + LANE]
            acc = jnp.concatenate([acc[:SUBLANE] + corr_scr[:, cs], acc[SUBLANE:]], axis=0)
            act = _silu_tanh(acc)
            blocks.append((act if scale is None else act * scale).astype(BF16))
        return jnp.concatenate(blocks, axis=1)

    def head_group(hs):
        q16 = {h: conv(h * dqk, None) for h in hs}
        k16 = {h: conv(nq + h * dqk, dqk ** -0.5) for h in hs}
        qc = {h: _dot(q16[h], caug[h].astype(BF16)) for h in hs}
        s_raw = {h: lax.dot_general(q16[h], k16[h], _NT, preferred_element_type=F32) for h in hs}

        m_t, w_state, w_end, decay, s16 = {}, {}, {}, {}, {}
        for h in hs:
            fc_c = fcum_c[:, heads + h:heads + h + 1]
            fc_r = fcum_r[heads + h:heads + h + 1, :]
            li_c = pre_c[:, h:h + 1]
            li_r = pre_r[h:h + 1, :]
            m_prev = m_scr[h, 0:1, 0:1]
            d = jnp.where(causal, fc_c - fc_r + li_r, NEG_BIG)
            inter = fc_c + m_prev
            m_t[h] = jnp.maximum(inter, jnp.max(d, axis=1, keepdims=True))
            w_state[h] = jnp.exp(inter - m_t[h])
            s16[h] = (s_raw[h] * jnp.exp(d - m_t[h])).astype(BF16)
            m_new = m_t[h][L - 1:L, :]
            fc_last = fc_c[L - 1:L, :]
            decay[h] = jnp.exp(fc_last + m_prev - m_new)
            w_end[h] = jnp.exp(fc_last - fc_c + li_c - m_new)
            m_scr[h] = jnp.broadcast_to(m_new, (SUBLANE, LANE))

        def v_of(h):
            return v_ref[:, h * dv:(h + 1) * dv]

        num = {h: _dot(s16[h], v_of(h)) for h in hs}
        dsum = {h: _dot(s16[h], ones_blk) for h in hs}
        kw = {h: k16[h] * w_end[h].astype(BF16) for h in hs}

        for h in hs:
            den = dsum[h][:, 0:1] + w_state[h] * qc[h][:, dv:dv + 1]
            rden = 1.0 / jnp.maximum(jnp.abs(den), jnp.exp(-m_t[h]))
            hb, ssq = [], 0.0
            for b0 in range(0, dv, LANE):
                blk = (num[h][:, b0:b0 + LANE] + w_state[h] * qc[h][:, b0:b0 + LANE]) * rden
                ssq = ssq + jnp.sum(blk * blk, axis=1, keepdims=True)
                hb.append(blk)
            rinv = lax.rsqrt(ssq * (1.0 / dv) + EPS)
            for i, b0 in enumerate(range(0, dv, LANE)):
                vs = slice(h * dv + b0, h * dv + b0 + LANE)
                gate = 0.5 + 0.5 * jnp.tanh(0.5 * og_ref[:, vs].astype(F32))
                y_ref[:, vs] = (hb[i] * rinv * g_ref[:, vs] * gate).astype(y_ref.dtype)

        for h in hs:
            caug[h, :, :dv] = decay[h] * caug[h, :, :dv] + lax.dot_general(
                kw[h], v_of(h), _TN, preferred_element_type=F32)
            caug[h, :, dv:] = decay[h] * caug[h, :, dv:] + lax.dot_general(
                kw[h], ones_blk, _TN, preferred_element_type=F32)

    for h0 in range(0, heads, MLSTM_GROUP):
        head_group(list(range(h0, min(h0 + MLSTM_GROUP, heads))))

    ubuf[...] = qk_ref[L - 2 * SUBLANE:L, :].astype(F32)[SUBLANE:, :]

    @pl.when(c_idx == nc - 1)
    def _fin():
        c_out[...] = caug[:, :, :dv]
        n_out[...] = caug[:, :, dv:]
        m_out[...] = m_scr[...]
        conv_out[...] = ubuf[...]


def _mlstm(p_f, qk_off, og_off, p_bf, v_off, p_if, b_i, b_f, conv_w, conv_b, g, state, c_acc, layer, batch, t):
    heads, dqk, dv = A_HEADS, A_DQK, A_DV
    nq, nv = heads * dqk, heads * dv
    m_rows = batch * t
    L = min(MLSTM_CHUNK, t)
    nc = t // L
    assert t % L == 0 and L >= 2 * SUBLANE and qk_off % (2 * nq) == 0 and og_off % nv == 0 and v_off % nv == 0
    has_state = state is not None

    ifr = p_if[:, :SUBLANE].reshape(m_rows // L, L, SUBLANE).transpose(0, 2, 1)
    bias = jnp.concatenate([b_i, b_f]).astype(F32)
    bias_c = jnp.zeros((1, LANE), F32).at[0, :2 * heads].set(bias)
    bias_r = jnp.zeros((SUBLANE, 1), F32).at[:2 * heads, 0].set(bias)

    def rows(width, off):
        return pl.BlockSpec((L, width), lambda b, c: (b * nc + c, off // width))

    def whole(shape):
        return pl.BlockSpec(shape, lambda b, c: (0,) * len(shape))

    in_specs = [rows(2 * nq, qk_off), rows(nv, v_off), rows(nv, og_off), rows(LANE, 0),
                pl.BlockSpec((None, SUBLANE, L), lambda b, c: (b * nc + c, 0, 0)),
                whole((1, LANE)), whole((SUBLANE, 1)), whole((CONV_W, 2 * nq)), whole((1, 2 * nq)),
                whole((1, nv))]
    args = [p_f, p_bf, p_f, p_if, ifr, bias_c, bias_r, conv_w, conv_b.reshape(1, 2 * nq), g.reshape(1, nv)]
    if has_state:
        c0, n0, m0, conv0 = state
        n0p = jnp.pad(n0[layer][..., None], ((0, 0), (0, 0), (0, 0), (0, LANE - 1)))
        m0p = jnp.broadcast_to(m0[layer][:, :, None, None], (batch, heads, SUBLANE, LANE))
        conv0p = jnp.pad(conv0[layer], ((0, 0), (SUBLANE - (CONV_W - 1), 0), (0, 0)))
        in_specs += [pl.BlockSpec((None, None, heads, dqk, dv), lambda b, c: (layer, b, 0, 0, 0)),
                     pl.BlockSpec((None, heads, dqk, LANE), lambda b, c: (b, 0, 0, 0)),
                     pl.BlockSpec((None, heads, SUBLANE, LANE), lambda b, c: (b, 0, 0, 0)),
                     pl.BlockSpec((None, SUBLANE, 2 * nq), lambda b, c: (b, 0, 0))]
        args += [c0, n0p, m0p, conv0p]

    if c_acc is None:
        c_acc = jnp.zeros((DEPTH, batch, heads, dqk, dv), F32)
    in_specs = [pl.BlockSpec(memory_space=pl.ANY)] + in_specs
    args = [c_acc] + args
    aliases = {0: 1}
    out_specs = [pl.BlockSpec((L, nv), lambda b, c: (b * nc + c, 0)),
                 pl.BlockSpec((None, None, heads, dqk, dv), lambda b, c: (layer, b, 0, 0, 0)),
                 pl.BlockSpec((None, heads, dqk, LANE), lambda b, c: (b, 0, 0, 0)),
                 pl.BlockSpec((None, heads, SUBLANE, LANE), lambda b, c: (b, 0, 0, 0)),
                 pl.BlockSpec((None, SUBLANE, 2 * nq), lambda b, c: (b, 0, 0))]
    out_shape = [jax.ShapeDtypeStruct((m_rows, nv), BF16),
                 jax.ShapeDtypeStruct((DEPTH, batch, heads, dqk, dv), F32),
                 jax.ShapeDtypeStruct((batch, heads, dqk, LANE), F32),
                 jax.ShapeDtypeStruct((batch, heads, SUBLANE, LANE), F32),
                 jax.ShapeDtypeStruct((batch, SUBLANE, 2 * nq), F32)]
    ya, c_new, n_new, m_new, conv_new = pl.pallas_call(
        functools.partial(_mlstm_kernel, L=L, nc=nc, heads=heads, dqk=dqk, dv=dv, has_state=has_state),
        grid=(batch, nc),
        in_specs=in_specs,
        out_specs=out_specs,
        out_shape=out_shape,
        input_output_aliases=aliases,
        scratch_shapes=[pltpu.VMEM((SUBLANE, 2 * nq), F32),
                        pltpu.VMEM((heads, dqk, dv + LANE), F32),
                        pltpu.VMEM((heads, SUBLANE, LANE), F32),
                        pltpu.VMEM((SUBLANE, 2 * nq), F32)],
        compiler_params=_params("parallel", "arbitrary"),
        name="mlstm",
    )(*args)
    return ya, c_new, n_new[..., 0], m_new[:, :, 0, 0], conv_new[:, SUBLANE - (CONV_W - 1):, :]


def _neg_abs(x):
    bits = lax.bitcast_convert_type(x, jnp.uint32) | jnp.uint32(0x80000000)
    return lax.bitcast_convert_type(bits, F32)


def _ref_rows(g_scr, L, bs, n):
    hs = bs // 2

    def bcast(r, rows):
        return jnp.broadcast_to(g_scr[r:r + 1, :], (rows, n))

    if bs >= 2 * SUBLANE:
        return jnp.concatenate([bcast(b0 + hs - 1, bs) for b0 in range(0, L, bs)], axis=0)
    sub = lax.broadcasted_iota(jnp.int32, (SUBLANE, n), 0)
    groups = []
    for g0 in range(0, L, SUBLANE):
        r0 = g0 + hs - 1
        val = bcast(r0, SUBLANE)
        for b0 in range(bs, SUBLANE, bs):
            val = jnp.where(sub >= b0, bcast(r0 + b0, SUBLANE), val)
        groups.append(val)
    return jnp.concatenate(groups, axis=0)


def _hgrn_kernel(*refs, L, nsub, nc, heads, dk, dv, layer, has_state):
    refs = refs[1:]
    if has_state:
        q_ref, f_ref, v_ref, gt_ref, lbraw_ref, g_ref, s0_ref, y_ref, s_out, st, lb_scr, g_scr = refs
    else:
        q_ref, f_ref, v_ref, gt_ref, lbraw_ref, g_ref, y_ref, s_out, st, lb_scr, g_scr = refs
    c_idx = pl.program_id(1)

    @pl.when(c_idx == 0)
    def _init():
        for h in range(heads):
            st[h] = s0_ref[h].T if has_state else jnp.zeros((dv, dk), F32)
        raw = lbraw_ref[...]
        e = jnp.exp(raw - jnp.max(raw, axis=0, keepdims=True))
        sm = e / jnp.sum(e, axis=0, keepdims=True)
        lb = jnp.zeros((1, heads * dk), F32)
        for j in range(1, layer + 1):
            lb = lb + sm[j:j + 1, :]
        lb_scr[...] = lb

    row = lax.broadcasted_iota(jnp.int32, (L, L), 0)
    col = lax.broadcasted_iota(jnp.int32, (L, L), 1)
    tri = jnp.where(row >= col, 1.0, 0.0).astype(BF16)
    diag = row == col
    levels = []
    bs = 2
    while bs <= L:
        hs, lg = bs // 2, bs.bit_length() - 1
        pair = ((row >> lg) == (col >> lg)) & ((row & (bs - 1)) >= hs) & ((col & (bs - 1)) < hs)
        levels.append((bs, pair))
        bs *= 2

    def nt(a, b):
        return lax.dot_general(a, b, _NT, preferred_element_type=F32)

    def head_group(hs, sub):
        rs = pl.ds(sub * L, L)
        kb16, qb16, gs, f16 = [], [], [], []
        for h in hs:
            kc = pl.ds(h * dk, dk)
            z = f_ref[rs, kc]
            lb = lb_scr[:, kc]
            t = jnp.exp(_neg_abs(z))
            r = 1.0 / (1.0 + t)
            pos = z >= 0.0
            f = jnp.where(pos, 1.0 + lb * t, lb + t) * r
            kb = (1.0 - lb) * (jnp.where(pos, t, 1.0) * r)
            qb = _silu_tanh(q_ref[rs, kc].astype(F32))
            kb16.append(kb.astype(BF16))
            qb16.append(qb.astype(BF16))
            f16.append(f.astype(BF16))
            gs.append(_dot(tri, jnp.concatenate(_split3(jnp.log(f) * LOG2E)[:2], axis=1)))
        g2, pend = [], []
        for u, h in enumerate(hs):
            g2.append(gs[u][:, :dk] + gs[u][:, dk:])
            g_scr[sub * heads + h] = g2[u]
            pend.append(nt(qb16[u], kb16[u]))
        a = [0.0] * len(hs)
        mask = diag
        for bs, pair in levels:
            new = []
            for u, h in enumerate(hs):
                if bs == 2:
                    new.append(nt(qb16[u] * f16[u], kb16[u]))
                    continue
                e = jnp.exp2(_neg_abs(g2[u] - _ref_rows(g_scr.at[sub * heads + h], L, bs, dk))).astype(BF16)
                new.append(nt(qb16[u] * e, kb16[u] * e))
            a = [jnp.where(mask, p, x) for p, x in zip(pend, a)]
            pend, mask = new, pair
        a = [jnp.where(mask, p, x) for p, x in zip(pend, a)]
        o = []
        for u, h in enumerate(hs):
            vc = pl.ds(h * dv, dv)
            qg = qb16[u] * jnp.exp2(g2[u]).astype(BF16)
            o.append((_dot(a[u].astype(BF16), v_ref[rs, vc]), nt(qg, st[h].astype(BF16))))
        upd = []
        for u, h in enumerate(hs):
            g_end = g2[u][L - 1:L, :]
            ke = kb16[u] * jnp.exp2(g_end - g2[u]).astype(BF16)
            upd.append(lax.dot_general(v_ref[rs, pl.ds(h * dv, dv)], ke, _TN, preferred_element_type=F32))
        for u, h in enumerate(hs):
            vc = pl.ds(h * dv, dv)
            y = _rms(o[u][0] + o[u][1]) * g_ref[...] * _silu_tanh(gt_ref[rs, vc].astype(F32))
            y_ref[rs, vc] = y.astype(y_ref.dtype)
        for u, h in enumerate(hs):
            st[h] = jnp.exp2(g2[u][L - 1:L, :]) * st[h] + upd[u]

    for sub in range(nsub):
        for h0 in range(0, heads, HGRN_UNROLL):
            head_group(list(range(h0, h0 + HGRN_UNROLL)), sub)

    @pl.when(c_idx == nc - 1)
    def _fin():
        for h in range(heads):
            s_out[h] = st[h].T


def _hgrn(p_f, q_off, f_off, gt_off, p_bf, v_off, lb_raw, g, s0, s_acc, layer, batch, t):
    heads, dk, dv = B_HEADS, B_DK, B_DV
    nk, nv = heads * dk, heads * dv
    L = min(HGRN_CHUNK, t)
    nsub = HGRN_SUB if t % (L * HGRN_SUB) == 0 else 1
    nc = t // (L * nsub)
    assert t % L == 0 and q_off % nk == 0 and f_off % nk == 0 and gt_off % nv == 0 and v_off % nv == 0
    assert heads % HGRN_UNROLL == 0
    has_state = s0 is not None

    def rows(width, off):
        return pl.BlockSpec((L * nsub, width), lambda b, c: (b * nc + c, off // width))

    in_specs = [rows(nk, q_off), rows(nk, f_off), rows(nv, v_off), rows(nv, gt_off),
                pl.BlockSpec(lb_raw.shape, lambda b, c: (0, 0)),
                pl.BlockSpec((1, dv), lambda b, c: (0, 0))]
    args = [p_bf, p_f, p_bf, p_bf, lb_raw, g.reshape(1, dv)]
    if has_state:
        in_specs.append(pl.BlockSpec((None, None, heads, dk, dv), lambda b, c: (layer, b, 0, 0, 0)))
        args.append(s0)
    if s_acc is None:
        s_acc = jnp.zeros((DEPTH, batch, heads, dk, dv), F32)
    in_specs = [pl.BlockSpec(memory_space=pl.ANY)] + in_specs
    args = [s_acc] + args
    aliases = {0: 1}
    return pl.pallas_call(
        functools.partial(_hgrn_kernel, L=L, nsub=nsub, nc=nc, heads=heads, dk=dk, dv=dv, layer=layer,
                          has_state=has_state),
        grid=(batch, nc),
        in_specs=in_specs,
        out_specs=[pl.BlockSpec((L * nsub, nv), lambda b, c: (b * nc + c, 0)),
                   pl.BlockSpec((None, None, heads, dk, dv), lambda b, c: (layer, b, 0, 0, 0))],
        out_shape=[jax.ShapeDtypeStruct((batch * t, nv), BF16),
                   jax.ShapeDtypeStruct((DEPTH, batch, heads, dk, dv), F32)],
        input_output_aliases=aliases,
        scratch_shapes=[pltpu.VMEM((heads, dv, dk), F32),
                        pltpu.VMEM((1, nk), F32),
                        pltpu.VMEM((nsub * heads, L, dk), F32)],
        compiler_params=_params("parallel", "arbitrary"),
        name="hgrn",
    )(*args)


def _regroup_kernel(w_ref, obf_ref, of_ref, oif_ref, *, src, order, n_if):
    o = 0
    for name in order:
        s, width = src[name]
        obf_ref[:, o:o + width] = w_ref[:, s:s + width].astype(BF16)
        o += width
    s, width = src["f_b"]
    of_ref[...] = w_ref[:, s:s + width].astype(BF16)
    s, _ = src["if_a"]
    lane = lax.broadcasted_iota(jnp.int32, oif_ref.shape, 1)
    oif_ref[...] = jnp.where(lane < n_if, w_ref[:, s:s + LANE], 0.0).astype(BF16)


def _prep_weights(w_in, w_proj_a, w_proj_b, w_proj_m, w_out, w_ffn_in, w_ffn_out, w_mem_kv):
    a_qk = 2 * A_HEADS * A_DQK
    a_v = A_HEADS * A_DV
    b_k = B_HEADS * B_DK
    b_v = B_HEADS * B_DV
    m_w = M_HEADS * M_HD
    depth, k, n_in = w_in.shape
    d = w_out.shape[1]
    src, pos = {}, 0
    for name, width in (("qk_a", a_qk), ("v_a", a_v), ("o_a", a_v), ("if_a", 2 * A_HEADS), ("q_b", b_k),
                        ("f_b", b_k), ("i_b", b_v), ("gt_b", b_v), ("q_m", m_w), ("gates", 3 * d)):
        src[name] = (pos, width)
        pos += width
    assert pos == n_in
    order = ("v_a", "i_b", "q_m", "qk_a", "o_a", "q_b", "gt_b", "gates")
    offs, n_bf = {}, 0
    for name in order:
        offs[name] = n_bf
        n_bf += src[name][1]
    tk = _tile(k, 64, 16)
    w_bf, w_f, w_if = pl.pallas_call(
        functools.partial(_regroup_kernel, src=src, order=order, n_if=2 * A_HEADS),
        grid=(depth, k // tk),
        in_specs=[pl.BlockSpec((None, tk, n_in), lambda l, i: (l, i, 0))],
        out_specs=[pl.BlockSpec((None, tk, n_bf), lambda l, i: (l, i, 0)),
                   pl.BlockSpec((None, tk, b_k), lambda l, i: (l, i, 0)),
                   pl.BlockSpec((None, tk, LANE), lambda l, i: (l, i, 0))],
        out_shape=[jax.ShapeDtypeStruct((depth, k, n_bf), BF16),
                   jax.ShapeDtypeStruct((depth, k, b_k), BF16),
                   jax.ShapeDtypeStruct((depth, k, LANE), BF16)],
        compiler_params=_params("parallel", "parallel"),
        name="regroup_w_in",
    )(w_in)
    return dict(w_bf=w_bf, w_f=w_f, w_if=w_if, offs=offs,
                w_proj_a=w_proj_a.astype(BF16), w_proj_b=w_proj_b.astype(BF16),
                w_proj_m=w_proj_m.astype(BF16), w_out=w_out.astype(BF16),
                w_ffn_in=w_ffn_in.astype(BF16), w_ffn_out=w_ffn_out.astype(BF16),
                w_mem_kv=w_mem_kv.astype(BF16))


def _trunk(x, memk, memv, state, W, P):
    batch, t, d = x.shape
    offs = W["offs"]
    x2 = x.reshape(batch * t, d)
    h = _rmsnorm(x2, P["g_mix"][0], BF16)
    new_conv, new_n, new_m = [], [], []
    ca = sb = None
    for l in range(DEPTH):
        p_bf = _matmul(h, W["w_bf"], l, BF16, tn=1024)
        p_f, p_if = _matmul_side(h, W["w_f"], W["w_if"], l)

        ya, ca, na, ma, cbuf = _mlstm(
            p_bf, offs["qk_a"], offs["o_a"], p_bf, offs["v_a"], p_if, P["b_igate"][l], P["b_fgate"][l],
            P["conv_w"][l], P["conv_b"][l], P["g_mlstm"][l],
            None if state is None else state[:4], ca, l, batch, t)
        yb, sb = _hgrn(p_f, offs["q_b"], 0, offs["gt_b"], p_bf, offs["i_b"], P["lb_raw"],
                       P["g_hgrn"][l], None if state is None else state[4], sb, l, batch, t)
        ym = _memattn(p_bf, offs["q_m"], memk, memv, l, batch, t)

        mix = _mix(ya, yb, ym, W["w_proj_a"], W["w_proj_b"], W["w_proj_m"], l, p_bf, offs["gates"], d)
        x2, h2 = _mm_res_norm(mix, W["w_out"], l, x2, P["g_ffn"][l], BF16, tm=512)
        act = _ffn_in(h2, W["w_ffn_in"], l)
        last = l == DEPTH - 1
        x2, h = _mm_res_norm(act, W["w_ffn_out"], l, x2, P["g_final"] if last else P["g_mix"][l + 1],
                             F32 if last else BF16, tm=256)
        new_conv.append(cbuf)
        new_n.append(na)
        new_m.append(ma)
    return (h.reshape(batch, t, d), jnp.stack(new_conv), ca, jnp.stack(new_n), jnp.stack(new_m), sb)


def kernel(x_prompt, x_sample, mem_prompt, cache_mem_k, cache_mem_v, state_mlstm_c, state_mlstm_n, state_mlstm_m, state_mlstm_conv, state_hgrn_s, g_mix, w_in, conv_w, conv_b, b_igate, b_fgate, g_mlstm, lb_raw, g_hgrn, g_mem, w_mem_kv, w_proj_a, w_proj_b, w_proj_m, w_out, g_ffn, w_ffn_in, w_ffn_out, g_final):
    W = _prep_weights(w_in, w_proj_a, w_proj_b, w_proj_m, w_out, w_ffn_in, w_ffn_out, w_mem_kv)
    P = dict(g_mix=g_mix, conv_w=conv_w, conv_b=conv_b, b_igate=b_igate, b_fgate=b_fgate, g_mlstm=g_mlstm,
             lb_raw=lb_raw, g_hgrn=g_hgrn, g_ffn=g_ffn, g_final=g_final)
    batch, n_mem, d = mem_prompt.shape
    dec_batch = x_sample.shape[0]
    m_w = M_HEADS * M_HD

    mem2 = mem_prompt.reshape(batch * n_mem, d)
    ks, vs = [], []
    for l in range(DEPTH):
        mem_n = _rmsnorm(mem2, g_mem[l], BF16)
        ks.append(_matmul(mem_n, W["w_mem_kv"], l, F32, col_off=0, n_cols=m_w))
        vs.append(_matmul(mem_n, W["w_mem_kv"], l, F32, col_off=m_w, n_cols=m_w))
    mem_shape = (DEPTH, batch, n_mem, M_HEADS, M_HD)
    mem_k = jnp.stack(ks).reshape(DEPTH, batch, n_mem, m_w)
    mem_v = jnp.stack(vs).reshape(DEPTH, batch, n_mem, m_w)

    y_p, conv_p, c_p, n_p, m_p, s_p = _trunk(x_prompt, mem_k, mem_v, None, W, P)
    mem_k, mem_v = mem_k.reshape(mem_shape), mem_v.reshape(mem_shape)

    state = (state_mlstm_c, state_mlstm_n, state_mlstm_m, state_mlstm_conv, state_hgrn_s)
    y_s, conv_s, c_s, n_s, m_s, s_s = _trunk(x_sample, cache_mem_k, cache_mem_v, state, W, P)

    return (y_p, y_s, mem_k, mem_v, conv_p, c_p, n_p, m_p, s_p, conv_s, c_s, n_s, m_s, s_s)
```

```python
import functools

import jax
import jax.numpy as jnp
from jax import lax
from jax.experimental import pallas as pl
from jax.experimental.pallas import tpu as pltpu

F32 = jnp.float32
BF16 = jnp.bfloat16

EPS = 1e-6
NEG_BIG = -1e30
DEPTH = 4
CONV_W = 4
A_HEADS = 4
A_DQK = 256
A_DV = 512
B_HEADS = 16
B_DK = 128
B_DV = 128
M_HEADS = 4
M_HD = 512

LANE = 128
SUBLANE = 8
VMEM_LIMIT = 56 * 1024 * 1024
MLSTM_CHUNK = 128
MLSTM_GROUP = 4
HGRN_CHUNK = 64
HGRN_UNROLL = 4
HGRN_SUB = 4
LOG2E = 1.4426950408889634

_NT = (((1,), (1,)), ((), ()))
_TN = (((0,), (0,)), ((), ()))


def _params(*sem):
    return pltpu.CompilerParams(dimension_semantics=sem, vmem_limit_bytes=VMEM_LIMIT)


def _tile(n, pref, align):
    best = None
    t = align
    while t <= min(n, pref):
        if n % t == 0:
            best = t
        t += align
    return best if best is not None else n


def _dot(a, b):
    return jnp.dot(a, b, preferred_element_type=F32)


def _log_sigmoid(x):
    return jnp.minimum(x, 0.0) - jnp.log1p(jnp.exp(-jnp.abs(x)))


def _silu(x):
    return x * jax.nn.sigmoid(x)


def _silu_tanh(x):
    hx = 0.5 * x
    return hx + hx * jnp.tanh(hx)


def _rms(x):
    return x * lax.rsqrt(jnp.mean(x * x, axis=-1, keepdims=True) + EPS)


def _rmsnorm_kernel(x_ref, g_ref, o_ref):
    o_ref[...] = (_rms(x_ref[...]) * g_ref[...]).astype(o_ref.dtype)


def _rmsnorm(x, g, out_dtype):
    m, d = x.shape
    tm = _tile(m, 512, 16)
    return pl.pallas_call(
        _rmsnorm_kernel,
        grid=(m // tm,),
        in_specs=[pl.BlockSpec((tm, d), lambda i: (i, 0)), pl.BlockSpec((1, d), lambda i: (0, 0))],
        out_specs=pl.BlockSpec((tm, d), lambda i: (i, 0)),
        out_shape=jax.ShapeDtypeStruct((m, d), out_dtype),
        compiler_params=_params("parallel"),
        name="rmsnorm",
    )(x, g.reshape(1, d))


def _matmul_kernel(a_ref, w_ref, o_ref):
    o_ref[...] = _dot(a_ref[...], w_ref[...]).astype(o_ref.dtype)


def _matmul(a, w, layer, out_dtype, *, col_off=0, n_cols=None, tm=2048, tn=512):
    m, k = a.shape
    n_cols = w.shape[2] - col_off if n_cols is None else n_cols
    tm = _tile(m, tm, 16)
    tn = _tile(n_cols, tn, LANE)
    assert col_off % tn == 0
    joff = col_off // tn
    return pl.pallas_call(
        _matmul_kernel,
        grid=(m // tm, n_cols // tn),
        in_specs=[pl.BlockSpec((tm, k), lambda i, j: (i, 0)),
                  pl.BlockSpec((None, k, tn), lambda i, j: (layer, 0, j + joff))],
        out_specs=pl.BlockSpec((tm, tn), lambda i, j: (i, j)),
        out_shape=jax.ShapeDtypeStruct((m, n_cols), out_dtype),
        compiler_params=_params("parallel", "arbitrary"),
        name="matmul",
    )(a, w)


def _matmul_side_kernel(a_ref, w_ref, ws_ref, o_ref, os_ref):
    a = a_ref[...]
    o_ref[...] = _dot(a, w_ref[...])

    @pl.when(pl.program_id(1) == 0)
    def _():
        os_ref[...] = _dot(a, ws_ref[...])


def _matmul_side(a, w, ws, layer, *, tm=2048, tn=512):
    m, k = a.shape
    n, ns = w.shape[2], ws.shape[2]
    tm = _tile(m, tm, 16)
    tn = _tile(n, tn, LANE)
    return pl.pallas_call(
        _matmul_side_kernel,
        grid=(m // tm, n // tn),
        in_specs=[pl.BlockSpec((tm, k), lambda i, j: (i, 0)),
                  pl.BlockSpec((None, k, tn), lambda i, j: (layer, 0, j)),
                  pl.BlockSpec((None, k, ns), lambda i, j: (layer, 0, 0))],
        out_specs=[pl.BlockSpec((tm, tn), lambda i, j: (i, j)),
                   pl.BlockSpec((tm, ns), lambda i, j: (i, 0))],
        out_shape=[jax.ShapeDtypeStruct((m, n), F32), jax.ShapeDtypeStruct((m, ns), F32)],
        compiler_params=_params("parallel", "arbitrary"),
        name="matmul_side",
    )(a, w, ws)


def _mm_res_norm_kernel(a_ref, w_ref, x_ref, g_ref, *out_refs):
    x = x_ref[...] + _dot(a_ref[...], w_ref[...])
    ho_ref = out_refs[-1]
    if len(out_refs) == 2:
        out_refs[0][...] = x
    ho_ref[...] = (_rms(x) * g_ref[...]).astype(ho_ref.dtype)


def _mm_res_norm(a, w, layer, x, g, norm_dtype, *, tm, keep_x=True):
    m, k = a.shape
    d = w.shape[2]
    tm = _tile(m, tm, 16)
    row_spec = pl.BlockSpec((tm, d), lambda i: (i, 0))
    outs = pl.pallas_call(
        _mm_res_norm_kernel,
        grid=(m // tm,),
        in_specs=[pl.BlockSpec((tm, k), lambda i: (i, 0)),
                  pl.BlockSpec((None, k, d), lambda i: (layer, 0, 0), pipeline_mode=pl.Buffered(1)),
                  row_spec,
                  pl.BlockSpec((1, d), lambda i: (0, 0))],
        out_specs=[row_spec] * (2 if keep_x else 1),
        out_shape=([jax.ShapeDtypeStruct((m, d), F32)] if keep_x else []) + [jax.ShapeDtypeStruct((m, d), norm_dtype)],
        compiler_params=_params("parallel"),
        name="mm_res_norm",
    )(a, w, x, g.reshape(1, d))
    return (outs[0], outs[1]) if keep_x else (None, outs[0])


def _ffn_in_kernel(h_ref, wg_ref, wu_ref, o_ref):
    h = h_ref[...]
    g = _dot(h, wg_ref[...])
    u = _dot(h, wu_ref[...])
    o_ref[...] = (_silu_tanh(g) * u).astype(o_ref.dtype)


def _ffn_in(h, w, layer, *, tm=1024, tn=512):
    m, d = h.shape
    dff = w.shape[2] // 2
    tm = _tile(m, tm, 16)
    tn = _tile(dff, tn, LANE)
    nj = dff // tn
    return pl.pallas_call(
        _ffn_in_kernel,
        grid=(m // tm, nj),
        in_specs=[pl.BlockSpec((tm, d), lambda i, j: (i, 0)),
                  pl.BlockSpec((None, d, tn), lambda i, j: (layer, 0, j)),
                  pl.BlockSpec((None, d, tn), lambda i, j: (layer, 0, j + nj))],
        out_specs=pl.BlockSpec((tm, tn), lambda i, j: (i, j)),
        out_shape=jax.ShapeDtypeStruct((m, dff), BF16),
        compiler_params=_params("parallel", "arbitrary"),
        name="ffn_in",
    )(h, w, w)


def _mix_kernel(ya_ref, yb_ref, ym_ref, wa_ref, wb_ref, wm_ref, ga_ref, gb_ref, gm_ref, o_ref):
    def gate(ref):
        return 0.5 + 0.5 * jnp.tanh(0.5 * ref[...].astype(F32))

    acc = gate(ga_ref) * _dot(ya_ref[...], wa_ref[...])
    acc = acc + gate(gb_ref) * _dot(yb_ref[...], wb_ref[...])
    acc = acc + gate(gm_ref) * _dot(ym_ref[...], wm_ref[...])
    o_ref[...] = acc.astype(o_ref.dtype)


def _mix(ya, yb, ym, wa, wb, wm, layer, p_f, gate_off, d, *, tm=1024, tn=512):
    m = ya.shape[0]
    tm = _tile(m, tm, 16)
    tn = _tile(d, tn, LANE)
    assert gate_off % tn == 0
    goff = gate_off // tn
    nj = d // tn

    def y_spec(y):
        return pl.BlockSpec((tm, y.shape[1]), lambda i, j: (i, 0))

    def w_spec(w):
        return pl.BlockSpec((None, w.shape[1], tn), lambda i, j: (layer, 0, j))

    def g_spec(which):
        return pl.BlockSpec((tm, tn), lambda i, j: (i, goff + which * nj + j))

    return pl.pallas_call(
        _mix_kernel,
        grid=(m // tm, nj),
        in_specs=[y_spec(ya), y_spec(yb), y_spec(ym), w_spec(wa), w_spec(wb), w_spec(wm),
                  g_spec(0), g_spec(1), g_spec(2)],
        out_specs=pl.BlockSpec((tm, tn), lambda i, j: (i, j)),
        out_shape=jax.ShapeDtypeStruct((m, d), BF16),
        compiler_params=_params("parallel", "arbitrary"),
        name="mix",
    )(ya, yb, ym, wa, wb, wm, p_f, p_f, p_f)


def _memattn_kernel(q_ref, k_ref, v_ref, o_ref, k16, v16, *dma, heads, hd, layer, nb):
    b = pl.program_id(0)

    if dma:
        kbuf, vbuf, sem = dma

        def copies(bb, slot):
            return [pltpu.make_async_copy(src.at[layer, bb, :, h, :], dst.at[slot, h], sem.at[slot, i * heads + h])
                    for i, (src, dst) in enumerate(((k_ref, kbuf), (v_ref, vbuf))) for h in range(heads)]

    @pl.when(pl.program_id(1) == 0)
    def _():
        if dma:
            @pl.when(b == 0)
            def _():
                for c in copies(0, 0):
                    c.start()

            @pl.when(b + 1 < nb)
            def _():
                for c in copies(b + 1, (b + 1) % 2):
                    c.start()

            for c in copies(b, b % 2):
                c.wait()
        for h in range(heads):
            if dma:
                k16[h] = kbuf[b % 2, h].astype(BF16)
                v16[h] = vbuf[b % 2, h].astype(BF16)
            else:
                k16[h] = k_ref[:, h * hd:(h + 1) * hd].astype(BF16)
                v16[h] = v_ref[:, h * hd:(h + 1) * hd].astype(BF16)

    for h in range(heads):
        sl = slice(h * hd, (h + 1) * hd)
        s = lax.dot_general(q_ref[:, sl], k16[h], _NT, preferred_element_type=F32) * (hd ** -0.5)
        p = jnp.exp(s - jnp.max(s, axis=-1, keepdims=True))
        p = p / jnp.sum(p, axis=-1, keepdims=True)
        o_ref[:, sl] = _dot(p.astype(BF16), v16[h]).astype(o_ref.dtype)


def _memattn(p_bf, q_off, memk, memv, layer, batch, t, *, tq=512):
    n_mem = memk.shape[2]
    heads, hd = M_HEADS, M_HD
    w = heads * hd
    tq = _tile(t, tq, 16)
    nt = t // tq
    assert q_off % w == 0
    qoff = q_off // w
    scratch = [pltpu.VMEM((heads, n_mem, hd), BF16), pltpu.VMEM((heads, n_mem, hd), BF16)]
    if memk.ndim == 5:
        mem_spec = pl.BlockSpec(memory_space=pl.ANY)
        scratch += [pltpu.VMEM((2, heads, n_mem, hd), F32), pltpu.VMEM((2, heads, n_mem, hd), F32),
                    pltpu.SemaphoreType.DMA((2, 2 * heads))]
        sem = ("arbitrary", "arbitrary")
    else:
        mem_spec = pl.BlockSpec((None, None, n_mem, w), lambda b, i: (layer, b, 0, 0))
        sem = ("parallel", "arbitrary")
    return pl.pallas_call(
        functools.partial(_memattn_kernel, heads=heads, hd=hd, layer=layer, nb=batch),
        grid=(batch, nt),
        in_specs=[pl.BlockSpec((tq, w), lambda b, i: (b * nt + i, qoff)), mem_spec, mem_spec],
        out_specs=pl.BlockSpec((tq, w), lambda b, i: (b * nt + i, 0)),
        out_shape=jax.ShapeDtypeStruct((batch * t, w), BF16),
        scratch_shapes=scratch,
        compiler_params=_params(*sem),
        name="memattn",
    )(p_bf, memk, memv)


def _split3(x):
    hi = x.astype(BF16)
    r = x - hi.astype(F32)
    mid = r.astype(BF16)
    lo = (r - mid.astype(F32)).astype(BF16)
    return hi, mid, lo


def _mlstm_kernel(*refs, L, nc, heads, dqk, dv, has_state):
    refs = refs[1:]
    if has_state:
        (qk_ref, v_ref, og_ref, ifc_ref, ifr_ref, bc_ref, br_ref, cw_ref, cb_ref, g_ref,
         c0_ref, n0_ref, m0_ref, conv0_ref,
         y_ref, c_out, n_out, m_out, conv_out, ubuf, caug, m_scr, corr_scr) = refs
    else:
        (qk_ref, v_ref, og_ref, ifc_ref, ifr_ref, bc_ref, br_ref, cw_ref, cb_ref, g_ref,
         y_ref, c_out, n_out, m_out, conv_out, ubuf, caug, m_scr, corr_scr) = refs
    c_idx = pl.program_id(1)
    nq = heads * dqk

    @pl.when(c_idx == 0)
    def _init():
        if has_state:
            ubuf[...] = conv0_ref[...]
            caug[:, :, :dv] = c0_ref[...]
            caug[:, :, dv:] = n0_ref[...]
            m_scr[...] = m0_ref[...]
        else:
            ubuf[...] = jnp.zeros((SUBLANE, 2 * nq), F32)
            caug[...] = jnp.zeros(caug.shape, F32)
            m_scr[...] = jnp.zeros(m_scr.shape, F32)

    pre_c = ifc_ref[...] + bc_ref[...]
    pre_r = ifr_ref[...] + br_ref[...]
    lf_c = _log_sigmoid(pre_c)
    lf_r = _log_sigmoid(pre_r)
    row = lax.broadcasted_iota(jnp.int32, (L, L), 0)
    col = lax.broadcasted_iota(jnp.int32, (L, L), 1)
    causal = row >= col
    tri = jnp.where(causal, 1.0, 0.0).astype(BF16)
    tri_t = jnp.where(row <= col, 1.0, 0.0).astype(BF16)
    fcum_c = sum(_dot(tri, part) for part in _split3(lf_c))
    fcum_r = sum(_dot(part, tri_t) for part in _split3(lf_r))

    shift = jnp.concatenate(
        [jnp.where(row - col == CONV_W - 1 - j, 1.0, 0.0).astype(BF16) for j in range(CONV_W - 1)], axis=0)
    sub = lax.broadcasted_iota(jnp.int32, (SUBLANE, 2 * nq), 0)
    corr = jnp.zeros((SUBLANE, 2 * nq), F32)
    for t0 in range(CONV_W - 1):
        r = sum(cw_ref[j:j + 1, :] * ubuf[SUBLANE - (CONV_W - 1) + t0 + j:SUBLANE - (CONV_W - 2) + t0 + j, :]
                for j in range(CONV_W - 1 - t0))
        corr = jnp.where(sub == t0, r, corr)
    corr_scr[...] = corr

    ones_blk = jnp.where(lax.broadcasted_iota(jnp.int32, (L, LANE), 1) == 0, 1.0, 0.0).astype(BF16)

    def conv(c0, scale):
        sh = _dot(shift, qk_ref[:, c0:c0 + dqk])
        blocks = []
        for b0 in range(0, dqk, LANE):
            cs = slice(c0 + b0, c0 + b0 + LANE)
            acc = cb_ref[:, cs] + cw_ref[CONV_W - 1:CONV_W, cs] * qk_ref[:, cs].astype(F32)
            for j in range(CONV_W - 1):
                acc = acc + cw_ref[j:j + 1, cs] * sh[j * L:(j + 1) * L, b0:b0 + LANE]
            acc = jnp.concatenate([acc[:SUBLANE] + corr_scr[:, cs], acc[SUBLANE:]], axis=0)
            act = _silu_tanh(acc)
            blocks.append((act if scale is None else act * scale).astype(BF16))
        return jnp.concatenate(blocks, axis=1)

    def head_group(hs):
        q16 = {h: conv(h * dqk, None) for h in hs}
        k16 = {h: conv(nq + h * dqk, dqk ** -0.5) for h in hs}
        qc = {h: _dot(q16[h], caug[h].astype(BF16)) for h in hs}
        s_raw = {h: lax.dot_general(q16[h], k16[h], _NT, preferred_element_type=F32) for h in hs}

        m_t, w_state, w_end, decay, s16 = {}, {}, {}, {}, {}
        for h in hs:
            fc_c = fcum_c[:, heads + h:heads + h + 1]
            fc_r = fcum_r[heads + h:heads + h + 1, :]
            li_c = pre_c[:, h:h + 1]
            li_r = pre_r[h:h + 1, :]
            m_prev = m_scr[h, 0:1, 0:1]
            d = jnp.where(causal, fc_c - fc_r + li_r, NEG_BIG)
            inter = fc_c + m_prev
            m_t[h] = jnp.maximum(inter, jnp.max(d, axis=1, keepdims=True))
            w_state[h] = jnp.exp(inter - m_t[h])
            s16[h] = (s_raw[h] * jnp.exp(d - m_t[h])).astype(BF16)
            m_new = m_t[h][L - 1:L, :]
            fc_last = fc_c[L - 1:L, :]
            decay[h] = jnp.exp(fc_last + m_prev - m_new)
            w_end[h] = jnp.exp(fc_last - fc_c + li_c - m_new)
            m_scr[h] = jnp.broadcast_to(m_new, (SUBLANE, LANE))

        def v_of(h):
            return v_ref[:, h * dv:(h + 1) * dv]

        num = {h: _dot(s16[h], v_of(h)) for h in hs}
        dsum = {h: _dot(s16[h], ones_blk) for h in hs}
        kw = {h: k16[h] * w_end[h].astype(BF16) for h in hs}

        for h in hs:
            den = dsum[h][:, 0:1] + w_state[h] * qc[h][:, dv:dv + 1]
            rden = 1.0 / jnp.maximum(jnp.abs(den), jnp.exp(-m_t[h]))
            hb, ssq = [], 0.0
            for b0 in range(0, dv, LANE):
                blk = (num[h][:, b0:b0 + LANE] + w_state[h] * qc[h][:, b0:b0 + LANE]) * rden
                ssq = ssq + jnp.sum(blk * blk, axis=1, keepdims=True)
                hb.append(blk)
            rinv = lax.rsqrt(ssq * (1.0 / dv) + EPS)
            for i, b0 in enumerate(range(0, dv, LANE)):
                vs = slice(h * dv + b0, h * dv + b0 + LANE)
                gate = 0.5 + 0.5 * jnp.tanh(0.5 * og_ref[:, vs].astype(F32))
                y_ref[:, vs] = (hb[i] * rinv * g_ref[:, vs] * gate).astype(y_ref.dtype)

        for h in hs:
            caug[h, :, :dv] = decay[h] * caug[h, :, :dv] + lax.dot_general(
                kw[h], v_of(h), _TN, preferred_element_type=F32)
            caug[h, :, dv:] = decay[h] * caug[h, :, dv:] + lax.dot_general(
                kw[h], ones_blk, _TN, preferred_element_type=F32)

    for h0 in range(0, heads, MLSTM_GROUP):
        head_group(list(range(h0, min(h0 + MLSTM_GROUP, heads))))

    ubuf[...] = qk_ref[L - 2 * SUBLANE:L, :].astype(F32)[SUBLANE:, :]

    @pl.when(c_idx == nc - 1)
    def _fin():
        c_out[...] = caug[:, :, :dv]
        n_out[...] = caug[:, :, dv:]
        m_out[...] = m_scr[...]
        conv_out[...] = ubuf[...]


def _mlstm(p_f, qk_off, og_off, p_bf, v_off, p_if, b_i, b_f, conv_w, conv_b, g, state, c_acc, layer, batch, t):
    heads, dqk, dv = A_HEADS, A_DQK, A_DV
    nq, nv = heads * dqk, heads * dv
    m_rows = batch * t
    L = min(MLSTM_CHUNK, t)
    nc = t // L
    assert t % L == 0 and L >= 2 * SUBLANE and qk_off % (2 * nq) == 0 and og_off % nv == 0 and v_off % nv == 0
    has_state = state is not None

    ifr = p_if[:, :SUBLANE].reshape(m_rows // L, L, SUBLANE).transpose(0, 2, 1)
    bias = jnp.concatenate([b_i, b_f]).astype(F32)
    bias_c = jnp.zeros((1, LANE), F32).at[0, :2 * heads].set(bias)
    bias_r = jnp.zeros((SUBLANE, 1), F32).at[:2 * heads, 0].set(bias)

    def rows(width, off):
        return pl.BlockSpec((L, width), lambda b, c: (b * nc + c, off // width))

    def whole(shape):
        return pl.BlockSpec(shape, lambda b, c: (0,) * len(shape))

    in_specs = [rows(2 * nq, qk_off), rows(nv, v_off), rows(nv, og_off), rows(LANE, 0),
                pl.BlockSpec((None, SUBLANE, L), lambda b, c: (b * nc + c, 0, 0)),
                whole((1, LANE)), whole((SUBLANE, 1)), whole((CONV_W, 2 * nq)), whole((1, 2 * nq)),
                whole((1, nv))]
    args = [p_f, p_bf, p_f, p_if, ifr, bias_c, bias_r, conv_w, conv_b.reshape(1, 2 * nq), g.reshape(1, nv)]
    if has_state:
        c0, n0, m0, conv0 = state
        n0p = jnp.pad(n0[layer][..., None], ((0, 0), (0, 0), (0, 0), (0, LANE - 1)))
        m0p = jnp.broadcast_to(m0[layer][:, :, None, None], (batch, heads, SUBLANE, LANE))
        conv0p = jnp.pad(conv0[layer], ((0, 0), (SUBLANE - (CONV_W - 1), 0), (0, 0)))
        in_specs += [pl.BlockSpec((None, None, heads, dqk, dv), lambda b, c: (layer, b, 0, 0, 0)),
                     pl.BlockSpec((None, heads, dqk, LANE), lambda b, c: (b, 0, 0, 0)),
                     pl.BlockSpec((None, heads, SUBLANE, LANE), lambda b, c: (b, 0, 0, 0)),
                     pl.BlockSpec((None, SUBLANE, 2 * nq), lambda b, c: (b, 0, 0))]
        args += [c0, n0p, m0p, conv0p]

    if c_acc is None:
        c_acc = jnp.zeros((DEPTH, batch, heads, dqk, dv), F32)
    in_specs = [pl.BlockSpec(memory_space=pl.ANY)] + in_specs
    args = [c_acc] + args
    aliases = {0: 1}
    out_specs = [pl.BlockSpec((L, nv), lambda b, c: (b * nc + c, 0)),
                 pl.BlockSpec((None, None, heads, dqk, dv), lambda b, c: (layer, b, 0, 0, 0)),
                 pl.BlockSpec((None, heads, dqk, LANE), lambda b, c: (b, 0, 0, 0)),
                 pl.BlockSpec((None, heads, SUBLANE, LANE), lambda b, c: (b, 0, 0, 0)),
                 pl.BlockSpec((None, SUBLANE, 2 * nq), lambda b, c: (b, 0, 0))]
    out_shape = [jax.ShapeDtypeStruct((m_rows, nv), BF16),
                 jax.ShapeDtypeStruct((DEPTH, batch, heads, dqk, dv), F32),
                 jax.ShapeDtypeStruct((batch, heads, dqk, LANE), F32),
                 jax.ShapeDtypeStruct((batch, heads, SUBLANE, LANE), F32),
                 jax.ShapeDtypeStruct((batch, SUBLANE, 2 * nq), F32)]
    ya, c_new, n_new, m_new, conv_new = pl.pallas_call(
        functools.partial(_mlstm_kernel, L=L, nc=nc, heads=heads, dqk=dqk, dv=dv, has_state=has_state),
        grid=(batch, nc),
        in_specs=in_specs,
        out_specs=out_specs,
        out_shape=out_shape,
        input_output_aliases=aliases,
        scratch_shapes=[pltpu.VMEM((SUBLANE, 2 * nq), F32),
                        pltpu.VMEM((heads, dqk, dv + LANE), F32),
                        pltpu.VMEM((heads, SUBLANE, LANE), F32),
                        pltpu.VMEM((SUBLANE, 2 * nq), F32)],
        compiler_params=_params("parallel", "arbitrary"),
        name="mlstm",
    )(*args)
    return ya, c_new, n_new[..., 0], m_new[:, :, 0, 0], conv_new[:, SUBLANE - (CONV_W - 1):, :]


def _neg_abs(x):
    bits = lax.bitcast_convert_type(x, jnp.uint32) | jnp.uint32(0x80000000)
    return lax.bitcast_convert_type(bits, F32)


def _ref_rows(g_scr, L, bs, n):
    hs = bs // 2

    def bcast(r, rows):
        return jnp.broadcast_to(g_scr[r:r + 1, :], (rows, n))

    if bs >= 2 * SUBLANE:
        return jnp.concatenate([bcast(b0 + hs - 1, bs) for b0 in range(0, L, bs)], axis=0)
    sub = lax.broadcasted_iota(jnp.int32, (SUBLANE, n), 0)
    groups = []
    for g0 in range(0, L, SUBLANE):
        r0 = g0 + hs - 1
        val = bcast(r0, SUBLANE)
        for b0 in range(bs, SUBLANE, bs):
            val = jnp.where(sub >= b0, bcast(r0 + b0, SUBLANE), val)
        groups.append(val)
    return jnp.concatenate(groups, axis=0)


def _hgrn_kernel(*refs, L, nsub, nc, heads, dk, dv, layer, has_state):
    refs = refs[1:]
    if has_state:
        q_ref, f_ref, v_ref, gt_ref, lbraw_ref, g_ref, s0_ref, y_ref, s_out, st, lb_scr, g_scr = refs
    else:
        q_ref, f_ref, v_ref, gt_ref, lbraw_ref, g_ref, y_ref, s_out, st, lb_scr, g_scr = refs
    c_idx = pl.program_id(1)

    @pl.when(c_idx == 0)
    def _init():
        for h in range(heads):
            st[h] = s0_ref[h].T if has_state else jnp.zeros((dv, dk), F32)
        raw = lbraw_ref[...]
        e = jnp.exp(raw - jnp.max(raw, axis=0, keepdims=True))
        sm = e / jnp.sum(e, axis=0, keepdims=True)
        lb = jnp.zeros((1, heads * dk), F32)
        for j in range(1, layer + 1):
            lb = lb + sm[j:j + 1, :]
        lb_scr[...] = lb

    row = lax.broadcasted_iota(jnp.int32, (L, L), 0)
    col = lax.broadcasted_iota(jnp.int32, (L, L), 1)
    tri = jnp.where(row >= col, 1.0, 0.0).astype(BF16)
    diag = row == col
    levels = []
    bs = 2
    while bs <= L:
        hs, lg = bs // 2, bs.bit_length() - 1
        pair = ((row >> lg) == (col >> lg)) & ((row & (bs - 1)) >= hs) & ((col & (bs - 1)) < hs)
        levels.append((bs, pair))
        bs *= 2

    def nt(a, b):
        return lax.dot_general(a, b, _NT, preferred_element_type=F32)

    def head_group(hs, sub):
        rs = pl.ds(sub * L, L)
        kb16, qb16, gs, f16 = [], [], [], []
        for h in hs:
            kc = pl.ds(h * dk, dk)
            z = f_ref[rs, kc]
            lb = lb_scr[:, kc]
            t = jnp.exp(_neg_abs(z))
            r = 1.0 / (1.0 + t)
            pos = z >= 0.0
            f = jnp.where(pos, 1.0 + lb * t, lb + t) * r
            kb = (1.0 - lb) * (jnp.where(pos, t, 1.0) * r)
            qb = _silu_tanh(q_ref[rs, kc].astype(F32))
            kb16.append(kb.astype(BF16))
            qb16.append(qb.astype(BF16))
            f16.append(f.astype(BF16))
            gs.append(_dot(tri, jnp.concatenate(_split3(jnp.log(f) * LOG2E)[:2], axis=1)))
        g2, pend = [], []
        for u, h in enumerate(hs):
            g2.append(gs[u][:, :dk] + gs[u][:, dk:])
            g_scr[sub * heads + h] = g2[u]
            pend.append(nt(qb16[u], kb16[u]))
        a = [0.0] * len(hs)
        mask = diag
        for bs, pair in levels:
            new = []
            for u, h in enumerate(hs):
                if bs == 2:
                    new.append(nt(qb16[u] * f16[u], kb16[u]))
                    continue
                e = jnp.exp2(_neg_abs(g2[u] - _ref_rows(g_scr.at[sub * heads + h], L, bs, dk))).astype(BF16)
                new.append(nt(qb16[u] * e, kb16[u] * e))
            a = [jnp.where(mask, p, x) for p, x in zip(pend, a)]
            pend, mask = new, pair
        a = [jnp.where(mask, p, x) for p, x in zip(pend, a)]
        o = []
        for u, h in enumerate(hs):
            vc = pl.ds(h * dv, dv)
            qg = qb16[u] * jnp.exp2(g2[u]).astype(BF16)
            o.append((_dot(a[u].astype(BF16), v_ref[rs, vc]), nt(qg, st[h].astype(BF16))))
        upd = []
        for u, h in enumerate(hs):
            g_end = g2[u][L - 1:L, :]
            ke = kb16[u] * jnp.exp2(g_end - g2[u]).astype(BF16)
            upd.append(lax.dot_general(v_ref[rs, pl.ds(h * dv, dv)], ke, _TN, preferred_element_type=F32))
        for u, h in enumerate(hs):
            vc = pl.ds(h * dv, dv)
            y = _rms(o[u][0] + o[u][1]) * g_ref[...] * _silu_tanh(gt_ref[rs, vc].astype(F32))
            y_ref[rs, vc] = y.astype(y_ref.dtype)
        for u, h in enumerate(hs):
            st[h] = jnp.exp2(g2[u][L - 1:L, :]) * st[h] + upd[u]

    for sub in range(nsub):
        for h0 in range(0, heads, HGRN_UNROLL):
            head_group(list(range(h0, h0 + HGRN_UNROLL)), sub)

    @pl.when(c_idx == nc - 1)
    def _fin():
        for h in range(heads):
            s_out[h] = st[h].T


def _hgrn(p_f, q_off, f_off, gt_off, p_bf, v_off, lb_raw, g, s0, s_acc, layer, batch, t):
    heads, dk, dv = B_HEADS, B_DK, B_DV
    nk, nv = heads * dk, heads * dv
    L = min(HGRN_CHUNK, t)
    nsub = HGRN_SUB if t % (L * HGRN_SUB) == 0 else 1
    nc = t // (L * nsub)
    assert t % L == 0 and q_off % nk == 0 and f_off % nk == 0 and gt_off % nv == 0 and v_off % nv == 0
    assert heads % HGRN_UNROLL == 0
    has_state = s0 is not None

    def rows(width, off):
        return pl.BlockSpec((L * nsub, width), lambda b, c: (b * nc + c, off // width))

    in_specs = [rows(nk, q_off), rows(nk, f_off), rows(nv, v_off), rows(nv, gt_off),
                pl.BlockSpec(lb_raw.shape, lambda b, c: (0, 0)),
                pl.BlockSpec((1, dv), lambda b, c: (0, 0))]
    args = [p_bf, p_f, p_bf, p_bf, lb_raw, g.reshape(1, dv)]
    if has_state:
        in_specs.append(pl.BlockSpec((None, None, heads, dk, dv), lambda b, c: (layer, b, 0, 0, 0)))
        args.append(s0)
    if s_acc is None:
        s_acc = jnp.zeros((DEPTH, batch, heads, dk, dv), F32)
    in_specs = [pl.BlockSpec(memory_space=pl.ANY)] + in_specs
    args = [s_acc] + args
    aliases = {0: 1}
    return pl.pallas_call(
        functools.partial(_hgrn_kernel, L=L, nsub=nsub, nc=nc, heads=heads, dk=dk, dv=dv, layer=layer,
                          has_state=has_state),
        grid=(batch, nc),
        in_specs=in_specs,
        out_specs=[pl.BlockSpec((L * nsub, nv), lambda b, c: (b * nc + c, 0)),
                   pl.BlockSpec((None, None, heads, dk, dv), lambda b, c: (layer, b, 0, 0, 0))],
        out_shape=[jax.ShapeDtypeStruct((batch * t, nv), BF16),
                   jax.ShapeDtypeStruct((DEPTH, batch, heads, dk, dv), F32)],
        input_output_aliases=aliases,
        scratch_shapes=[pltpu.VMEM((heads, dv, dk), F32),
                        pltpu.VMEM((1, nk), F32),
                        pltpu.VMEM((nsub * heads, L, dk), F32)],
        compiler_params=_params("parallel", "arbitrary"),
        name="hgrn",
    )(*args)


def _regroup_kernel(wt_ref, o_ref):
    o_ref[...] = wt_ref[0].T.astype(BF16)


def _regroup(wt, sections, tn):
    depth, _, k = wt.shape
    plan, o = [], 0
    for s, width in sections:
        assert width % tn == 0
        plan.append((o // tn, (o + width) // tn, s))
        o += width
    aligned = all(s % SUBLANE == 0 for s, _ in sections)

    def src_row(j):
        r = 0
        for lo, hi, s in plan:
            r = jnp.where((j >= lo) & (j < hi), s + (j - lo) * tn, r)
        return pl.multiple_of(r, SUBLANE) if aligned else r

    return pl.pallas_call(
        _regroup_kernel,
        grid=(depth, o // tn),
        in_specs=[pl.BlockSpec((pl.Element(1), pl.Element(tn), pl.Element(k)),
                               lambda l, j: (l, src_row(j), 0))],
        out_specs=pl.BlockSpec((None, k, tn), lambda l, j: (l, 0, j)),
        out_shape=jax.ShapeDtypeStruct((depth, k, o), BF16),
        compiler_params=_params("parallel", "parallel"),
        name="regroup_w_in",
    )(wt)


def _prep_weights(w_in, w_proj_a, w_proj_b, w_proj_m, w_out, w_ffn_in, w_ffn_out, w_mem_kv):
    a_qk = 2 * A_HEADS * A_DQK
    a_v = A_HEADS * A_DV
    b_k = B_HEADS * B_DK
    b_v = B_HEADS * B_DV
    m_w = M_HEADS * M_HD
    depth, k, n_in = w_in.shape
    d = w_out.shape[1]
    src, pos = {}, 0
    for name, width in (("qk_a", a_qk), ("v_a", a_v), ("o_a", a_v), ("if_a", 2 * A_HEADS), ("q_b", b_k),
                        ("f_b", b_k), ("i_b", b_v), ("gt_b", b_v), ("q_m", m_w), ("gates", 3 * d)):
        src[name] = (pos, width)
        pos += width
    assert pos == n_in
    order = ("v_a", "i_b", "q_m", "qk_a", "o_a", "q_b", "gt_b", "gates")
    offs, n_bf = {}, 0
    for name in order:
        offs[name] = n_bf
        n_bf += src[name][1]
    wt = jnp.swapaxes(w_in, 1, 2)
    tn = 512
    while any(src[name][1] % tn for name in order + ("f_b",)):
        tn //= 2
    w_bf = _regroup(wt, [src[name] for name in order], tn)
    w_f = _regroup(wt, [src["f_b"]], tn)
    s_if, n_if = src["if_a"]
    w_if = jnp.pad(w_in[:, :, s_if:s_if + n_if], ((0, 0), (0, 0), (0, LANE - n_if))).astype(BF16)
    return dict(w_bf=w_bf, w_f=w_f, w_if=w_if, offs=offs,
                w_proj_a=w_proj_a.astype(BF16), w_proj_b=w_proj_b.astype(BF16),
                w_proj_m=w_proj_m.astype(BF16), w_out=w_out.astype(BF16),
                w_ffn_in=w_ffn_in.astype(BF16), w_ffn_out=w_ffn_out.astype(BF16),
                w_mem_kv=w_mem_kv.astype(BF16))


def _trunk(x, memk, memv, state, W, P):
    batch, t, d = x.shape
    offs = W["offs"]
    x2 = x.reshape(batch * t, d)
    h = _rmsnorm(x2, P["g_mix"][0], BF16)
    new_conv, new_n, new_m = [], [], []
    ca = sb = None
    for l in range(DEPTH):
        p_bf = _matmul(h, W["w_bf"], l, BF16, tn=1024)
        p_f, p_if = _matmul_side(h, W["w_f"], W["w_if"], l)

        ya, ca, na, ma, cbuf = _mlstm(
            p_bf, offs["qk_a"], offs["o_a"], p_bf, offs["v_a"], p_if, P["b_igate"][l], P["b_fgate"][l],
            P["conv_w"][l], P["conv_b"][l], P["g_mlstm"][l],
            None if state is None else state[:4], ca, l, batch, t)
        yb, sb = _hgrn(p_f, offs["q_b"], 0, offs["gt_b"], p_bf, offs["i_b"], P["lb_raw"],
                       P["g_hgrn"][l], None if state is None else state[4], sb, l, batch, t)
        ym = _memattn(p_bf, offs["q_m"], memk, memv, l, batch, t)

        mix = _mix(ya, yb, ym, W["w_proj_a"], W["w_proj_b"], W["w_proj_m"], l, p_bf, offs["gates"], d)
        x2, h2 = _mm_res_norm(mix, W["w_out"], l, x2, P["g_ffn"][l], BF16, tm=512)
        act = _ffn_in(h2, W["w_ffn_in"], l)
        last = l == DEPTH - 1
        x2, h = _mm_res_norm(act, W["w_ffn_out"], l, x2, P["g_final"] if last else P["g_mix"][l + 1],
                             F32 if last else BF16, tm=256, keep_x=not last)
        new_conv.append(cbuf)
        new_n.append(na)
        new_m.append(ma)
    return (h.reshape(batch, t, d), jnp.stack(new_conv), ca, jnp.stack(new_n), jnp.stack(new_m), sb)


def kernel(x_prompt, x_sample, mem_prompt, cache_mem_k, cache_mem_v, state_mlstm_c, state_mlstm_n, state_mlstm_m, state_mlstm_conv, state_hgrn_s, g_mix, w_in, conv_w, conv_b, b_igate, b_fgate, g_mlstm, lb_raw, g_hgrn, g_mem, w_mem_kv, w_proj_a, w_proj_b, w_proj_m, w_out, g_ffn, w_ffn_in, w_ffn_out, g_final):
    W = _prep_weights(w_in, w_proj_a, w_proj_b, w_proj_m, w_out, w_ffn_in, w_ffn_out, w_mem_kv)
    P = dict(g_mix=g_mix, conv_w=conv_w, conv_b=conv_b, b_igate=b_igate, b_fgate=b_fgate, g_mlstm=g_mlstm,
             lb_raw=lb_raw, g_hgrn=g_hgrn, g_ffn=g_ffn, g_final=g_final)
    batch, n_mem, d = mem_prompt.shape
    dec_batch = x_sample.shape[0]
    m_w = M_HEADS * M_HD

    mem2 = mem_prompt.reshape(batch * n_mem, d)
    ks, vs = [], []
    for l in range(DEPTH):
        mem_n = _rmsnorm(mem2, g_mem[l], BF16)
        ks.append(_matmul(mem_n, W["w_mem_kv"], l, F32, col_off=0, n_cols=m_w))
        vs.append(_matmul(mem_n, W["w_mem_kv"], l, F32, col_off=m_w, n_cols=m_w))
    mem_shape = (DEPTH, batch, n_mem, M_HEADS, M_HD)
    mem_k = jnp.stack(ks).reshape(DEPTH, batch, n_mem, m_w)
    mem_v = jnp.stack(vs).reshape(DEPTH, batch, n_mem, m_w)

    y_p, conv_p, c_p, n_p, m_p, s_p = _trunk(x_prompt, mem_k, mem_v, None, W, P)
    mem_k, mem_v = mem_k.reshape(mem_shape), mem_v.reshape(mem_shape)

    state = (state_mlstm_c, state_mlstm_n, state_mlstm_m, state_mlstm_conv, state_hgrn_s)
    y_s, conv_s, c_s, n_s, m_s, s_s = _trunk(x_sample, cache_mem_k, cache_mem_v, state, W, P)

    return (y_p, y_s, mem_k, mem_v, conv_p, c_p, n_p, m_p, s_p, conv_s, c_s, n_s, m_s, s_s)
```

```python
import functools

import jax
import jax.numpy as jnp
from jax import lax
from jax.experimental import pallas as pl
from jax.experimental.pallas import tpu as pltpu

F32 = jnp.float32
BF16 = jnp.bfloat16

EPS = 1e-6
NEG_BIG = -1e30
DEPTH = 4
CONV_W = 4
A_HEADS = 4
A_DQK = 256
A_DV = 512
B_HEADS = 16
B_DK = 128
B_DV = 128
M_HEADS = 4
M_HD = 512

LANE = 128
SUBLANE = 8
VMEM_LIMIT = 56 * 1024 * 1024
MLSTM_CHUNK = 256
MLSTM_GROUP_ROWS = 256
HGRN_CHUNK = 128
HGRN_UNROLL = 4
HGRN_SUB = 4
LOG2E = 1.4426950408889634

_NT = (((1,), (1,)), ((), ()))
_TN = (((0,), (0,)), ((), ()))


def _params(*sem):
    return pltpu.CompilerParams(dimension_semantics=sem, vmem_limit_bytes=VMEM_LIMIT)


def _tile(n, pref, align):
    best = None
    t = align
    while t <= min(n, pref):
        if n % t == 0:
            best = t
        t += align
    return best if best is not None else n


def _dot(a, b):
    return jnp.dot(a, b, preferred_element_type=F32)


def _log_sigmoid(x):
    return jnp.minimum(x, 0.0) - jnp.log1p(jnp.exp(-jnp.abs(x)))


def _silu(x):
    return x * jax.nn.sigmoid(x)


def _silu_tanh(x):
    hx = 0.5 * x
    return hx + hx * jnp.tanh(hx)


def _rms(x):
    return x * lax.rsqrt(jnp.mean(x * x, axis=-1, keepdims=True) + EPS)


def _rmsnorm_kernel(x_ref, g_ref, o_ref):
    o_ref[...] = (_rms(x_ref[...]) * g_ref[...]).astype(o_ref.dtype)


def _rmsnorm(x, g, out_dtype):
    m, d = x.shape
    tm = _tile(m, 512, 16)
    return pl.pallas_call(
        _rmsnorm_kernel,
        grid=(m // tm,),
        in_specs=[pl.BlockSpec((tm, d), lambda i: (i, 0)), pl.BlockSpec((1, d), lambda i: (0, 0))],
        out_specs=pl.BlockSpec((tm, d), lambda i: (i, 0)),
        out_shape=jax.ShapeDtypeStruct((m, d), out_dtype),
        compiler_params=_params("parallel"),
        name="rmsnorm",
    )(x, g.reshape(1, d))


def _matmul_kernel(a_ref, w_ref, o_ref):
    o_ref[...] = _dot(a_ref[...], w_ref[...]).astype(o_ref.dtype)


def _matmul(a, w, layer, out_dtype, *, col_off=0, n_cols=None, tm=2048, tn=512):
    m, k = a.shape
    n_cols = w.shape[2] - col_off if n_cols is None else n_cols
    tm = _tile(m, tm, 16)
    tn = _tile(n_cols, tn, LANE)
    assert col_off % tn == 0
    joff = col_off // tn
    return pl.pallas_call(
        _matmul_kernel,
        grid=(m // tm, n_cols // tn),
        in_specs=[pl.BlockSpec((tm, k), lambda i, j: (i, 0)),
                  pl.BlockSpec((None, k, tn), lambda i, j: (layer, 0, j + joff))],
        out_specs=pl.BlockSpec((tm, tn), lambda i, j: (i, j)),
        out_shape=jax.ShapeDtypeStruct((m, n_cols), out_dtype),
        compiler_params=_params("parallel", "arbitrary"),
        name="matmul",
    )(a, w)


def _matmul_side_kernel(a_ref, w_ref, ws_ref, o_ref, os_ref):
    a = a_ref[...]
    o_ref[...] = _dot(a, w_ref[...])

    @pl.when(pl.program_id(1) == 0)
    def _():
        os_ref[...] = _dot(a, ws_ref[...])


def _matmul_side(a, w, ws, layer, *, tm=2048, tn=512):
    m, k = a.shape
    n, ns = w.shape[2], ws.shape[2]
    tm = _tile(m, tm, 16)
    tn = _tile(n, tn, LANE)
    return pl.pallas_call(
        _matmul_side_kernel,
        grid=(m // tm, n // tn),
        in_specs=[pl.BlockSpec((tm, k), lambda i, j: (i, 0)),
                  pl.BlockSpec((None, k, tn), lambda i, j: (layer, 0, j)),
                  pl.BlockSpec((None, k, ns), lambda i, j: (layer, 0, 0))],
        out_specs=[pl.BlockSpec((tm, tn), lambda i, j: (i, j)),
                   pl.BlockSpec((tm, ns), lambda i, j: (i, 0))],
        out_shape=[jax.ShapeDtypeStruct((m, n), F32), jax.ShapeDtypeStruct((m, ns), F32)],
        compiler_params=_params("parallel", "arbitrary"),
        name="matmul_side",
    )(a, w, ws)


def _mm_res_norm_kernel(a_ref, w_ref, x_ref, g_ref, *out_refs):
    x = x_ref[...] + _dot(a_ref[...], w_ref[...])
    ho_ref = out_refs[-1]
    if len(out_refs) == 2:
        out_refs[0][...] = x
    ho_ref[...] = (_rms(x) * g_ref[...]).astype(ho_ref.dtype)


def _mm_res_norm(a, w, layer, x, g, norm_dtype, *, tm, keep_x=True):
    m, k = a.shape
    d = w.shape[2]
    tm = _tile(m, tm, 16)
    row_spec = pl.BlockSpec((tm, d), lambda i: (i, 0))
    outs = pl.pallas_call(
        _mm_res_norm_kernel,
        grid=(m // tm,),
        in_specs=[pl.BlockSpec((tm, k), lambda i: (i, 0)),
                  pl.BlockSpec((None, k, d), lambda i: (layer, 0, 0), pipeline_mode=pl.Buffered(1)),
                  row_spec,
                  pl.BlockSpec((1, d), lambda i: (0, 0))],
        out_specs=[row_spec] * (2 if keep_x else 1),
        out_shape=([jax.ShapeDtypeStruct((m, d), F32)] if keep_x else []) + [jax.ShapeDtypeStruct((m, d), norm_dtype)],
        compiler_params=_params("parallel"),
        name="mm_res_norm",
    )(a, w, x, g.reshape(1, d))
    return (outs[0], outs[1]) if keep_x else (None, outs[0])


def _ffn_in_kernel(h_ref, wg_ref, wu_ref, o_ref):
    h = h_ref[...]
    g = _dot(h, wg_ref[...])
    u = _dot(h, wu_ref[...])
    o_ref[...] = (_silu_tanh(g) * u).astype(o_ref.dtype)


def _ffn_in(h, w, layer, *, tm=1024, tn=512):
    m, d = h.shape
    dff = w.shape[2] // 2
    tm = _tile(m, tm, 16)
    tn = _tile(dff, tn, LANE)
    nj = dff // tn
    return pl.pallas_call(
        _ffn_in_kernel,
        grid=(m // tm, nj),
        in_specs=[pl.BlockSpec((tm, d), lambda i, j: (i, 0)),
                  pl.BlockSpec((None, d, tn), lambda i, j: (layer, 0, j)),
                  pl.BlockSpec((None, d, tn), lambda i, j: (layer, 0, j + nj))],
        out_specs=pl.BlockSpec((tm, tn), lambda i, j: (i, j)),
        out_shape=jax.ShapeDtypeStruct((m, dff), BF16),
        compiler_params=_params("parallel", "arbitrary"),
        name="ffn_in",
    )(h, w, w)


def _mix_kernel(ya_ref, yb_ref, ym_ref, wa_ref, wb_ref, wm_ref, ga_ref, gb_ref, gm_ref, o_ref):
    def gate(ref):
        return 0.5 + 0.5 * jnp.tanh(0.5 * ref[...].astype(F32))

    acc = gate(ga_ref) * _dot(ya_ref[...], wa_ref[...])
    acc = acc + gate(gb_ref) * _dot(yb_ref[...], wb_ref[...])
    acc = acc + gate(gm_ref) * _dot(ym_ref[...], wm_ref[...])
    o_ref[...] = acc.astype(o_ref.dtype)


def _mix(ya, yb, ym, wa, wb, wm, layer, p_f, gate_off, d, *, tm=1024, tn=512):
    m = ya.shape[0]
    tm = _tile(m, tm, 16)
    tn = _tile(d, tn, LANE)
    assert gate_off % tn == 0
    goff = gate_off // tn
    nj = d // tn

    def y_spec(y):
        return pl.BlockSpec((tm, y.shape[1]), lambda i, j: (i, 0))

    def w_spec(w):
        return pl.BlockSpec((None, w.shape[1], tn), lambda i, j: (layer, 0, j))

    def g_spec(which):
        return pl.BlockSpec((tm, tn), lambda i, j: (i, goff + which * nj + j))

    return pl.pallas_call(
        _mix_kernel,
        grid=(m // tm, nj),
        in_specs=[y_spec(ya), y_spec(yb), y_spec(ym), w_spec(wa), w_spec(wb), w_spec(wm),
                  g_spec(0), g_spec(1), g_spec(2)],
        out_specs=pl.BlockSpec((tm, tn), lambda i, j: (i, j)),
        out_shape=jax.ShapeDtypeStruct((m, d), BF16),
        compiler_params=_params("parallel", "arbitrary"),
        name="mix",
    )(ya, yb, ym, wa, wb, wm, p_f, p_f, p_f)


def _memattn_kernel(q_ref, k_ref, v_ref, o_ref, k16, v16, *dma, heads, hd, layer, nb):
    b = pl.program_id(0)

    if dma:
        kbuf, vbuf, sem = dma

        def copies(bb, slot):
            return [pltpu.make_async_copy(src.at[layer, bb, :, h, :], dst.at[slot, h], sem.at[slot, i * heads + h])
                    for i, (src, dst) in enumerate(((k_ref, kbuf), (v_ref, vbuf))) for h in range(heads)]

    @pl.when(pl.program_id(1) == 0)
    def _():
        if dma:
            @pl.when(b == 0)
            def _():
                for c in copies(0, 0):
                    c.start()

            @pl.when(b + 1 < nb)
            def _():
                for c in copies(b + 1, (b + 1) % 2):
                    c.start()

            for c in copies(b, b % 2):
                c.wait()
        for h in range(heads):
            if dma:
                k16[h] = kbuf[b % 2, h].astype(BF16)
                v16[h] = vbuf[b % 2, h].astype(BF16)
            else:
                k16[h] = k_ref[:, h * hd:(h + 1) * hd].astype(BF16)
                v16[h] = v_ref[:, h * hd:(h + 1) * hd].astype(BF16)

    for h in range(heads):
        sl = slice(h * hd, (h + 1) * hd)
        s = lax.dot_general(q_ref[:, sl], k16[h], _NT, preferred_element_type=F32) * (hd ** -0.5)
        p = jnp.exp(s - jnp.max(s, axis=-1, keepdims=True))
        p = p / jnp.sum(p, axis=-1, keepdims=True)
        o_ref[:, sl] = _dot(p.astype(BF16), v16[h]).astype(o_ref.dtype)


def _memattn(p_bf, q_off, memk, memv, layer, batch, t, *, tq=512):
    n_mem = memk.shape[2]
    heads, hd = M_HEADS, M_HD
    w = heads * hd
    tq = _tile(t, tq, 16)
    nt = t // tq
    assert q_off % w == 0
    qoff = q_off // w
    scratch = [pltpu.VMEM((heads, n_mem, hd), BF16), pltpu.VMEM((heads, n_mem, hd), BF16)]
    if memk.ndim == 5:
        mem_spec = pl.BlockSpec(memory_space=pl.ANY)
        scratch += [pltpu.VMEM((2, heads, n_mem, hd), F32), pltpu.VMEM((2, heads, n_mem, hd), F32),
                    pltpu.SemaphoreType.DMA((2, 2 * heads))]
        sem = ("arbitrary", "arbitrary")
    else:
        mem_spec = pl.BlockSpec((None, None, n_mem, w), lambda b, i: (layer, b, 0, 0))
        sem = ("parallel", "arbitrary")
    return pl.pallas_call(
        functools.partial(_memattn_kernel, heads=heads, hd=hd, layer=layer, nb=batch),
        grid=(batch, nt),
        in_specs=[pl.BlockSpec((tq, w), lambda b, i: (b * nt + i, qoff)), mem_spec, mem_spec],
        out_specs=pl.BlockSpec((tq, w), lambda b, i: (b * nt + i, 0)),
        out_shape=jax.ShapeDtypeStruct((batch * t, w), BF16),
        scratch_shapes=scratch,
        compiler_params=_params(*sem),
        name="memattn",
    )(p_bf, memk, memv)


def _split3(x):
    hi = x.astype(BF16)
    r = x - hi.astype(F32)
    mid = r.astype(BF16)
    lo = (r - mid.astype(F32)).astype(BF16)
    return hi, mid, lo


def _mlstm_kernel(*refs, L, nc, heads, dqk, dv, has_state):
    refs = refs[1:]
    if has_state:
        (qk_ref, v_ref, og_ref, ifc_ref, ifr_ref, bc_ref, br_ref, cw_ref, cb_ref, g_ref,
         c0_ref, n0_ref, m0_ref, conv0_ref,
         y_ref, c_out, n_out, m_out, conv_out, ubuf, caug, m_scr, corr_scr) = refs
    else:
        (qk_ref, v_ref, og_ref, ifc_ref, ifr_ref, bc_ref, br_ref, cw_ref, cb_ref, g_ref,
         y_ref, c_out, n_out, m_out, conv_out, ubuf, caug, m_scr, corr_scr) = refs
    c_idx = pl.program_id(1)
    nq = heads * dqk

    @pl.when(c_idx == 0)
    def _init():
        if has_state:
            ubuf[...] = conv0_ref[...]
            caug[:, :, :dv] = c0_ref[...]
            caug[:, :, dv:] = n0_ref[...]
            m_scr[...] = m0_ref[...]
        else:
            ubuf[...] = jnp.zeros((SUBLANE, 2 * nq), F32)
            caug[...] = jnp.zeros(caug.shape, F32)
            m_scr[...] = jnp.zeros(m_scr.shape, F32)

    pre_c = ifc_ref[...] + bc_ref[...]
    pre_r = ifr_ref[...] + br_ref[...]
    lf_c = _log_sigmoid(pre_c)
    lf_r = _log_sigmoid(pre_r)
    row = lax.broadcasted_iota(jnp.int32, (L, L), 0)
    col = lax.broadcasted_iota(jnp.int32, (L, L), 1)
    causal = row >= col
    tri = jnp.where(causal, 1.0, 0.0).astype(BF16)
    tri_t = jnp.where(row <= col, 1.0, 0.0).astype(BF16)
    fcum_c = sum(_dot(tri, part) for part in _split3(lf_c))
    fcum_r = sum(_dot(part, tri_t) for part in _split3(lf_r))

    shift = jnp.concatenate(
        [jnp.where(row - col == CONV_W - 1 - j, 1.0, 0.0).astype(BF16) for j in range(CONV_W - 1)], axis=0)
    sub = lax.broadcasted_iota(jnp.int32, (SUBLANE, 2 * nq), 0)
    corr = jnp.zeros((SUBLANE, 2 * nq), F32)
    for t0 in range(CONV_W - 1):
        r = sum(cw_ref[j:j + 1, :] * ubuf[SUBLANE - (CONV_W - 1) + t0 + j:SUBLANE - (CONV_W - 2) + t0 + j, :]
                for j in range(CONV_W - 1 - t0))
        corr = jnp.where(sub == t0, r, corr)
    corr_scr[...] = corr

    ones_blk = jnp.where(lax.broadcasted_iota(jnp.int32, (L, LANE), 1) == 0, 1.0, 0.0).astype(BF16)

    def conv(c0, scale):
        sh = _dot(shift, qk_ref[:, c0:c0 + dqk])
        blocks = []
        for b0 in range(0, dqk, LANE):
            cs = slice(c0 + b0, c0 + b0 + LANE)
            acc = cb_ref[:, cs] + cw_ref[CONV_W - 1:CONV_W, cs] * qk_ref[:, cs].astype(F32)
            for j in range(CONV_W - 1):
                acc = acc + cw_ref[j:j + 1, cs] * sh[j * L:(j + 1) * L, b0:b0 + LANE]
            acc = jnp.concatenate([acc[:SUBLANE] + corr_scr[:, cs], acc[SUBLANE:]], axis=0)
            act = _silu_tanh(acc)
            blocks.append((act if scale is None else act * scale).astype(BF16))
        return jnp.concatenate(blocks, axis=1)

    def head_group(hs):
        q16 = {h: conv(h * dqk, None) for h in hs}
        k16 = {h: conv(nq + h * dqk, dqk ** -0.5) for h in hs}
        qc = {h: _dot(q16[h], caug[h].astype(BF16)) for h in hs}
        s_raw = {h: lax.dot_general(q16[h], k16[h], _NT, preferred_element_type=F32) for h in hs}

        m_t, w_state, w_end, decay, s16 = {}, {}, {}, {}, {}
        for h in hs:
            fc_c = fcum_c[:, heads + h:heads + h + 1]
            fc_r = fcum_r[heads + h:heads + h + 1, :]
            li_c = pre_c[:, h:h + 1]
            li_r = pre_r[h:h + 1, :]
            m_prev = m_scr[h, 0:1, 0:1]
            d = jnp.where(causal, fc_c - fc_r + li_r, NEG_BIG)
            inter = fc_c + m_prev
            m_t[h] = jnp.maximum(inter, jnp.max(d, axis=1, keepdims=True))
            w_state[h] = jnp.exp(inter - m_t[h])
            s16[h] = (s_raw[h] * jnp.exp(d - m_t[h])).astype(BF16)
            m_new = m_t[h][L - 1:L, :]
            fc_last = fc_c[L - 1:L, :]
            decay[h] = jnp.exp(fc_last + m_prev - m_new)
            w_end[h] = jnp.exp(fc_last - fc_c + li_c - m_new)
            m_scr[h] = jnp.broadcast_to(m_new, (SUBLANE, LANE))

        def v_of(h):
            return v_ref[:, h * dv:(h + 1) * dv]

        num = {h: _dot(s16[h], v_of(h)) for h in hs}
        dsum = {h: _dot(s16[h], ones_blk) for h in hs}
        kw = {h: k16[h] * w_end[h].astype(BF16) for h in hs}

        for h in hs:
            den = dsum[h][:, 0:1] + w_state[h] * qc[h][:, dv:dv + 1]
            rden = 1.0 / jnp.maximum(jnp.abs(den), jnp.exp(-m_t[h]))
            hb, ssq = [], 0.0
            for b0 in range(0, dv, LANE):
                blk = (num[h][:, b0:b0 + LANE] + w_state[h] * qc[h][:, b0:b0 + LANE]) * rden
                ssq = ssq + jnp.sum(blk * blk, axis=1, keepdims=True)
                hb.append(blk)
            rinv = lax.rsqrt(ssq * (1.0 / dv) + EPS)
            for i, b0 in enumerate(range(0, dv, LANE)):
                vs = slice(h * dv + b0, h * dv + b0 + LANE)
                gate = 0.5 + 0.5 * jnp.tanh(0.5 * og_ref[:, vs].astype(F32))
                y_ref[:, vs] = (hb[i] * rinv * g_ref[:, vs] * gate).astype(y_ref.dtype)

        for h in hs:
            caug[h, :, :dv] = decay[h] * caug[h, :, :dv] + lax.dot_general(
                kw[h], v_of(h), _TN, preferred_element_type=F32)
            caug[h, :, dv:] = decay[h] * caug[h, :, dv:] + lax.dot_general(
                kw[h], ones_blk, _TN, preferred_element_type=F32)

    group = max(1, min(heads, MLSTM_GROUP_ROWS // L))
    for h0 in range(0, heads, group):
        head_group(list(range(h0, min(h0 + group, heads))))

    ubuf[...] = qk_ref[L - 2 * SUBLANE:L, :].astype(F32)[SUBLANE:, :]

    @pl.when(c_idx == nc - 1)
    def _fin():
        c_out[...] = caug[:, :, :dv]
        n_out[...] = caug[:, :, dv:]
        m_out[...] = m_scr[...]
        conv_out[...] = ubuf[...]


def _mlstm(p_f, qk_off, og_off, p_bf, v_off, p_if, b_i, b_f, conv_w, conv_b, g, state, c_acc, layer, batch, t):
    heads, dqk, dv = A_HEADS, A_DQK, A_DV
    nq, nv = heads * dqk, heads * dv
    m_rows = batch * t
    L = min(MLSTM_CHUNK, t)
    nc = t // L
    assert t % L == 0 and L >= 2 * SUBLANE and qk_off % (2 * nq) == 0 and og_off % nv == 0 and v_off % nv == 0
    has_state = state is not None

    ifr = p_if[:, :SUBLANE].reshape(m_rows // L, L, SUBLANE).transpose(0, 2, 1)
    bias = jnp.concatenate([b_i, b_f]).astype(F32)
    bias_c = jnp.zeros((1, LANE), F32).at[0, :2 * heads].set(bias)
    bias_r = jnp.zeros((SUBLANE, 1), F32).at[:2 * heads, 0].set(bias)

    def rows(width, off):
        return pl.BlockSpec((L, width), lambda b, c: (b * nc + c, off // width))

    def whole(shape):
        return pl.BlockSpec(shape, lambda b, c: (0,) * len(shape))

    in_specs = [rows(2 * nq, qk_off), rows(nv, v_off), rows(nv, og_off), rows(LANE, 0),
                pl.BlockSpec((None, SUBLANE, L), lambda b, c: (b * nc + c, 0, 0)),
                whole((1, LANE)), whole((SUBLANE, 1)), whole((CONV_W, 2 * nq)), whole((1, 2 * nq)),
                whole((1, nv))]
    args = [p_f, p_bf, p_f, p_if, ifr, bias_c, bias_r, conv_w, conv_b.reshape(1, 2 * nq), g.reshape(1, nv)]
    if has_state:
        c0, n0, m0, conv0 = state
        n0p = jnp.pad(n0[layer][..., None], ((0, 0), (0, 0), (0, 0), (0, LANE - 1)))
        m0p = jnp.broadcast_to(m0[layer][:, :, None, None], (batch, heads, SUBLANE, LANE))
        conv0p = jnp.pad(conv0[layer], ((0, 0), (SUBLANE - (CONV_W - 1), 0), (0, 0)))
        in_specs += [pl.BlockSpec((None, None, heads, dqk, dv), lambda b, c: (layer, b, 0, 0, 0)),
                     pl.BlockSpec((None, heads, dqk, LANE), lambda b, c: (b, 0, 0, 0)),
                     pl.BlockSpec((None, heads, SUBLANE, LANE), lambda b, c: (b, 0, 0, 0)),
                     pl.BlockSpec((None, SUBLANE, 2 * nq), lambda b, c: (b, 0, 0))]
        args += [c0, n0p, m0p, conv0p]

    if c_acc is None:
        c_acc = jnp.zeros((DEPTH, batch, heads, dqk, dv), F32)
    in_specs = [pl.BlockSpec(memory_space=pl.ANY)] + in_specs
    args = [c_acc] + args
    aliases = {0: 1}
    out_specs = [pl.BlockSpec((L, nv), lambda b, c: (b * nc + c, 0)),
                 pl.BlockSpec((None, None, heads, dqk, dv), lambda b, c: (layer, b, 0, 0, 0)),
                 pl.BlockSpec((None, heads, dqk, LANE), lambda b, c: (b, 0, 0, 0)),
                 pl.BlockSpec((None, heads, SUBLANE, LANE), lambda b, c: (b, 0, 0, 0)),
                 pl.BlockSpec((None, SUBLANE, 2 * nq), lambda b, c: (b, 0, 0))]
    out_shape = [jax.ShapeDtypeStruct((m_rows, nv), BF16),
                 jax.ShapeDtypeStruct((DEPTH, batch, heads, dqk, dv), F32),
                 jax.ShapeDtypeStruct((batch, heads, dqk, LANE), F32),
                 jax.ShapeDtypeStruct((batch, heads, SUBLANE, LANE), F32),
                 jax.ShapeDtypeStruct((batch, SUBLANE, 2 * nq), F32)]
    ya, c_new, n_new, m_new, conv_new = pl.pallas_call(
        functools.partial(_mlstm_kernel, L=L, nc=nc, heads=heads, dqk=dqk, dv=dv, has_state=has_state),
        grid=(batch, nc),
        in_specs=in_specs,
        out_specs=out_specs,
        out_shape=out_shape,
        input_output_aliases=aliases,
        scratch_shapes=[pltpu.VMEM((SUBLANE, 2 * nq), F32),
                        pltpu.VMEM((heads, dqk, dv + LANE), F32),
                        pltpu.VMEM((heads, SUBLANE, LANE), F32),
                        pltpu.VMEM((SUBLANE, 2 * nq), F32)],
        compiler_params=_params("parallel", "arbitrary"),
        name="mlstm",
    )(*args)
    return ya, c_new, n_new[..., 0], m_new[:, :, 0, 0], conv_new[:, SUBLANE - (CONV_W - 1):, :]


def _neg_abs(x):
    bits = lax.bitcast_convert_type(x, jnp.uint32) | jnp.uint32(0x80000000)
    return lax.bitcast_convert_type(bits, F32)


def _ref_rows(g_scr, L, bs, n):
    hs = bs // 2

    def bcast(r, rows):
        return jnp.broadcast_to(g_scr[r:r + 1, :], (rows, n))

    if bs >= 2 * SUBLANE:
        return jnp.concatenate([bcast(b0 + hs - 1, bs) for b0 in range(0, L, bs)], axis=0)
    sub = lax.broadcasted_iota(jnp.int32, (SUBLANE, n), 0)
    groups = []
    for g0 in range(0, L, SUBLANE):
        r0 = g0 + hs - 1
        val = bcast(r0, SUBLANE)
        for b0 in range(bs, SUBLANE, bs):
            val = jnp.where(sub >= b0, bcast(r0 + b0, SUBLANE), val)
        groups.append(val)
    return jnp.concatenate(groups, axis=0)


def _hgrn_kernel(*refs, L, nsub, nc, heads, dk, dv, layer, has_state):
    refs = refs[1:]
    if has_state:
        q_ref, f_ref, v_ref, gt_ref, lbraw_ref, g_ref, s0_ref, y_ref, s_out, st, lb_scr, g_scr = refs
    else:
        q_ref, f_ref, v_ref, gt_ref, lbraw_ref, g_ref, y_ref, s_out, st, lb_scr, g_scr = refs
    c_idx = pl.program_id(1)

    @pl.when(c_idx == 0)
    def _init():
        for h in range(heads):
            st[h] = s0_ref[h].T if has_state else jnp.zeros((dv, dk), F32)
        raw = lbraw_ref[...]
        e = jnp.exp(raw - jnp.max(raw, axis=0, keepdims=True))
        sm = e / jnp.sum(e, axis=0, keepdims=True)
        lb = jnp.zeros((1, heads * dk), F32)
        for j in range(1, layer + 1):
            lb = lb + sm[j:j + 1, :]
        lb_scr[...] = lb

    row = lax.broadcasted_iota(jnp.int32, (L, L), 0)
    col = lax.broadcasted_iota(jnp.int32, (L, L), 1)
    tri = jnp.where(row >= col, 1.0, 0.0).astype(BF16)
    diag = row == col
    levels = []
    bs = 2
    while bs <= L:
        hs, lg = bs // 2, bs.bit_length() - 1
        pair = ((row >> lg) == (col >> lg)) & ((row & (bs - 1)) >= hs) & ((col & (bs - 1)) < hs)
        levels.append((bs, pair))
        bs *= 2

    def nt(a, b):
        return lax.dot_general(a, b, _NT, preferred_element_type=F32)

    def head_group(hs, sub):
        rs = pl.ds(sub * L, L)
        kb16, qb16, gs, f16 = [], [], [], []
        for h in hs:
            kc = pl.ds(h * dk, dk)
            z = f_ref[rs, kc]
            lb = lb_scr[:, kc]
            t = jnp.exp(_neg_abs(z))
            r = 1.0 / (1.0 + t)
            pos = z >= 0.0
            f = jnp.where(pos, 1.0 + lb * t, lb + t) * r
            kb = (1.0 - lb) * (jnp.where(pos, t, 1.0) * r)
            qb = _silu_tanh(q_ref[rs, kc].astype(F32))
            kb16.append(kb.astype(BF16))
            qb16.append(qb.astype(BF16))
            f16.append(f.astype(BF16))
            gs.append(_dot(tri, jnp.concatenate(_split3(jnp.log(f) * LOG2E)[:2], axis=1)))
        g2, pend = [], []
        for u, h in enumerate(hs):
            g2.append(gs[u][:, :dk] + gs[u][:, dk:])
            g_scr[sub * heads + h] = g2[u]
            pend.append(nt(qb16[u], kb16[u]))
        a = [0.0] * len(hs)
        mask = diag
        for bs, pair in levels:
            new = []
            for u, h in enumerate(hs):
                if bs == 2:
                    new.append(nt(qb16[u] * f16[u], kb16[u]))
                    continue
                e = jnp.exp2(_neg_abs(g2[u] - _ref_rows(g_scr.at[sub * heads + h], L, bs, dk))).astype(BF16)
                new.append(nt(qb16[u] * e, kb16[u] * e))
            a = [jnp.where(mask, p, x) for p, x in zip(pend, a)]
            pend, mask = new, pair
        a = [jnp.where(mask, p, x) for p, x in zip(pend, a)]
        o = []
        for u, h in enumerate(hs):
            vc = pl.ds(h * dv, dv)
            qg = qb16[u] * jnp.exp2(g2[u]).astype(BF16)
            o.append((_dot(a[u].astype(BF16), v_ref[rs, vc]), nt(qg, st[h].astype(BF16))))
        upd = []
        for u, h in enumerate(hs):
            g_end = g2[u][L - 1:L, :]
            ke = kb16[u] * jnp.exp2(g_end - g2[u]).astype(BF16)
            upd.append(lax.dot_general(v_ref[rs, pl.ds(h * dv, dv)], ke, _TN, preferred_element_type=F32))
        for u, h in enumerate(hs):
            vc = pl.ds(h * dv, dv)
            y = _rms(o[u][0] + o[u][1]) * g_ref[...] * _silu_tanh(gt_ref[rs, vc].astype(F32))
            y_ref[rs, vc] = y.astype(y_ref.dtype)
        for u, h in enumerate(hs):
            st[h] = jnp.exp2(g2[u][L - 1:L, :]) * st[h] + upd[u]

    for sub in range(nsub):
        for h0 in range(0, heads, HGRN_UNROLL):
            head_group(list(range(h0, h0 + HGRN_UNROLL)), sub)

    @pl.when(c_idx == nc - 1)
    def _fin():
        for h in range(heads):
            s_out[h] = st[h].T


def _hgrn(p_f, q_off, f_off, gt_off, p_bf, v_off, lb_raw, g, s0, s_acc, layer, batch, t):
    heads, dk, dv = B_HEADS, B_DK, B_DV
    nk, nv = heads * dk, heads * dv
    L = min(HGRN_CHUNK, t)
    nsub = HGRN_SUB if t % (L * HGRN_SUB) == 0 else 1
    nc = t // (L * nsub)
    assert t % L == 0 and q_off % nk == 0 and f_off % nk == 0 and gt_off % nv == 0 and v_off % nv == 0
    assert heads % HGRN_UNROLL == 0
    has_state = s0 is not None

    def rows(width, off):
        return pl.BlockSpec((L * nsub, width), lambda b, c: (b * nc + c, off // width))

    in_specs = [rows(nk, q_off), rows(nk, f_off), rows(nv, v_off), rows(nv, gt_off),
                pl.BlockSpec(lb_raw.shape, lambda b, c: (0, 0)),
                pl.BlockSpec((1, dv), lambda b, c: (0, 0))]
    args = [p_bf, p_f, p_bf, p_bf, lb_raw, g.reshape(1, dv)]
    if has_state:
        in_specs.append(pl.BlockSpec((None, None, heads, dk, dv), lambda b, c: (layer, b, 0, 0, 0)))
        args.append(s0)
    if s_acc is None:
        s_acc = jnp.zeros((DEPTH, batch, heads, dk, dv), F32)
    in_specs = [pl.BlockSpec(memory_space=pl.ANY)] + in_specs
    args = [s_acc] + args
    aliases = {0: 1}
    return pl.pallas_call(
        functools.partial(_hgrn_kernel, L=L, nsub=nsub, nc=nc, heads=heads, dk=dk, dv=dv, layer=layer,
                          has_state=has_state),
        grid=(batch, nc),
        in_specs=in_specs,
        out_specs=[pl.BlockSpec((L * nsub, nv), lambda b, c: (b * nc + c, 0)),
                   pl.BlockSpec((None, None, heads, dk, dv), lambda b, c: (layer, b, 0, 0, 0))],
        out_shape=[jax.ShapeDtypeStruct((batch * t, nv), BF16),
                   jax.ShapeDtypeStruct((DEPTH, batch, heads, dk, dv), F32)],
        input_output_aliases=aliases,
        scratch_shapes=[pltpu.VMEM((heads, dv, dk), F32),
                        pltpu.VMEM((1, nk), F32),
                        pltpu.VMEM((nsub * heads, L, dk), F32)],
        compiler_params=_params("parallel", "arbitrary"),
        name="hgrn",
    )(*args)


def _regroup_kernel(wt_ref, o_ref):
    o_ref[...] = wt_ref[0].T.astype(BF16)


def _regroup(wt, sections, tn):
    depth, _, k = wt.shape
    plan, o = [], 0
    for s, width in sections:
        assert width % tn == 0
        plan.append((o // tn, (o + width) // tn, s))
        o += width
    aligned = all(s % SUBLANE == 0 for s, _ in sections)

    def src_row(j):
        r = 0
        for lo, hi, s in plan:
            r = jnp.where((j >= lo) & (j < hi), s + (j - lo) * tn, r)
        return pl.multiple_of(r, SUBLANE) if aligned else r

    return pl.pallas_call(
        _regroup_kernel,
        grid=(depth, o // tn),
        in_specs=[pl.BlockSpec((pl.Element(1), pl.Element(tn), pl.Element(k)),
                               lambda l, j: (l, src_row(j), 0))],
        out_specs=pl.BlockSpec((None, k, tn), lambda l, j: (l, 0, j)),
        out_shape=jax.ShapeDtypeStruct((depth, k, o), BF16),
        compiler_params=_params("parallel", "parallel"),
        name="regroup_w_in",
    )(wt)


def _prep_weights(w_in, w_proj_a, w_proj_b, w_proj_m, w_out, w_ffn_in, w_ffn_out, w_mem_kv):
    a_qk = 2 * A_HEADS * A_DQK
    a_v = A_HEADS * A_DV
    b_k = B_HEADS * B_DK
    b_v = B_HEADS * B_DV
    m_w = M_HEADS * M_HD
    depth, k, n_in = w_in.shape
    d = w_out.shape[1]
    src, pos = {}, 0
    for name, width in (("qk_a", a_qk), ("v_a", a_v), ("o_a", a_v), ("if_a", 2 * A_HEADS), ("q_b", b_k),
                        ("f_b", b_k), ("i_b", b_v), ("gt_b", b_v), ("q_m", m_w), ("gates", 3 * d)):
        src[name] = (pos, width)
        pos += width
    assert pos == n_in
    order = ("v_a", "i_b", "q_m", "qk_a", "o_a", "q_b", "gt_b", "gates")
    offs, n_bf = {}, 0
    for name in order:
        offs[name] = n_bf
        n_bf += src[name][1]
    wt = jnp.swapaxes(w_in, 1, 2)
    tn = 512
    while any(src[name][1] % tn for name in order + ("f_b",)):
        tn //= 2
    w_bf = _regroup(wt, [src[name] for name in order], tn)
    w_f = _regroup(wt, [src["f_b"]], tn)
    s_if, n_if = src["if_a"]
    w_if = jnp.pad(w_in[:, :, s_if:s_if + n_if], ((0, 0), (0, 0), (0, LANE - n_if))).astype(BF16)
    return dict(w_bf=w_bf, w_f=w_f, w_if=w_if, offs=offs,
                w_proj_a=w_proj_a.astype(BF16), w_proj_b=w_proj_b.astype(BF16),
                w_proj_m=w_proj_m.astype(BF16), w_out=w_out.astype(BF16),
                w_ffn_in=w_ffn_in.astype(BF16), w_ffn_out=w_ffn_out.astype(BF16),
                w_mem_kv=w_mem_kv.astype(BF16))


def _trunk(x, memk, memv, state, W, P):
    batch, t, d = x.shape
    offs = W["offs"]
    x2 = x.reshape(batch * t, d)
    h = _rmsnorm(x2, P["g_mix"][0], BF16)
    new_conv, new_n, new_m = [], [], []
    ca = sb = None
    for l in range(DEPTH):
        p_bf = _matmul(h, W["w_bf"], l, BF16, tn=1024)
        p_f, p_if = _matmul_side(h, W["w_f"], W["w_if"], l)

        ya, ca, na, ma, cbuf = _mlstm(
            p_bf, offs["qk_a"], offs["o_a"], p_bf, offs["v_a"], p_if, P["b_igate"][l], P["b_fgate"][l],
            P["conv_w"][l], P["conv_b"][l], P["g_mlstm"][l],
            None if state is None else state[:4], ca, l, batch, t)
        yb, sb = _hgrn(p_f, offs["q_b"], 0, offs["gt_b"], p_bf, offs["i_b"], P["lb_raw"],
                       P["g_hgrn"][l], None if state is None else state[4], sb, l, batch, t)
        ym = _memattn(p_bf, offs["q_m"], memk, memv, l, batch, t)

        mix = _mix(ya, yb, ym, W["w_proj_a"], W["w_proj_b"], W["w_proj_m"], l, p_bf, offs["gates"], d)
        x2, h2 = _mm_res_norm(mix, W["w_out"], l, x2, P["g_ffn"][l], BF16, tm=512)
        act = _ffn_in(h2, W["w_ffn_in"], l)
        last = l == DEPTH - 1
        x2, h = _mm_res_norm(act, W["w_ffn_out"], l, x2, P["g_final"] if last else P["g_mix"][l + 1],
                             F32 if last else BF16, tm=256, keep_x=not last)
        new_conv.append(cbuf)
        new_n.append(na)
        new_m.append(ma)
    return (h.reshape(batch, t, d), jnp.stack(new_conv), ca, jnp.stack(new_n), jnp.stack(new_m), sb)


def kernel(x_prompt, x_sample, mem_prompt, cache_mem_k, cache_mem_v, state_mlstm_c, state_mlstm_n, state_mlstm_m, state_mlstm_conv, state_hgrn_s, g_mix, w_in, conv_w, conv_b, b_igate, b_fgate, g_mlstm, lb_raw, g_hgrn, g_mem, w_mem_kv, w_proj_a, w_proj_b, w_proj_m, w_out, g_ffn, w_ffn_in, w_ffn_out, g_final):
    W = _prep_weights(w_in, w_proj_a, w_proj_b, w_proj_m, w_out, w_ffn_in, w_ffn_out, w_mem_kv)
    P = dict(g_mix=g_mix, conv_w=conv_w, conv_b=conv_b, b_igate=b_igate, b_fgate=b_fgate, g_mlstm=g_mlstm,
             lb_raw=lb_raw, g_hgrn=g_hgrn, g_ffn=g_ffn, g_final=g_final)
    batch, n_mem, d = mem_prompt.shape
    dec_batch = x_sample.shape[0]
    m_w = M_HEADS * M_HD

    mem2 = mem_prompt.reshape(batch * n_mem, d)
    ks, vs = [], []
    for l in range(DEPTH):
        mem_n = _rmsnorm(mem2, g_mem[l], BF16)
        ks.append(_matmul(mem_n, W["w_mem_kv"], l, F32, col_off=0, n_cols=m_w))
        vs.append(_matmul(mem_n, W["w_mem_kv"], l, F32, col_off=m_w, n_cols=m_w))
    mem_shape = (DEPTH, batch, n_mem, M_HEADS, M_HD)
    mem_k = jnp.stack(ks).reshape(DEPTH, batch, n_mem, m_w)
    mem_v = jnp.stack(vs).reshape(DEPTH, batch, n_mem, m_w)

    y_p, conv_p, c_p, n_p, m_p, s_p = _trunk(x_prompt, mem_k, mem_v, None, W, P)
    mem_k, mem_v = mem_k.reshape(mem_shape), mem_v.reshape(mem_shape)

    state = (state_mlstm_c, state_mlstm_n, state_mlstm_m, state_mlstm_conv, state_hgrn_s)
    y_s, conv_s, c_s, n_s, m_s, s_s = _trunk(x_sample, cache_mem_k, cache_mem_v, state, W, P)

    return (y_p, y_s, mem_k, mem_v, conv_p, c_p, n_p, m_p, s_p, conv_s, c_s, n_s, m_s, s_s)
```

```python
import functools

import jax
import jax.numpy as jnp
from jax import lax
from jax.experimental import pallas as pl
from jax.experimental.pallas import tpu as pltpu

F32 = jnp.float32
BF16 = jnp.bfloat16

EPS = 1e-6
NEG_BIG = -1e30
DEPTH = 4
CONV_W = 4
A_HEADS = 4
A_DQK = 256
A_DV = 512
B_HEADS = 16
B_DK = 128
B_DV = 128
M_HEADS = 4
M_HD = 512

LANE = 128
SUBLANE = 8
VMEM_LIMIT = 56 * 1024 * 1024
MLSTM_CHUNK = 256
MLSTM_GROUP_ROWS = 256
HGRN_CHUNK = 128
HGRN_UNROLL = 4
HGRN_SUB = 4
LOG2E = 1.4426950408889634

_NT = (((1,), (1,)), ((), ()))
_TN = (((0,), (0,)), ((), ()))


def _params(*sem):
    return pltpu.CompilerParams(dimension_semantics=sem, vmem_limit_bytes=VMEM_LIMIT)


def _tile(n, pref, align):
    best = None
    t = align
    while t <= min(n, pref):
        if n % t == 0:
            best = t
        t += align
    return best if best is not None else n


def _dot(a, b):
    return jnp.dot(a, b, preferred_element_type=F32)


def _log_sigmoid(x):
    return jnp.minimum(x, 0.0) - jnp.log1p(jnp.exp(-jnp.abs(x)))


def _silu_tanh(x):
    hx = 0.5 * x
    return hx + hx * jnp.tanh(hx)


def _rms(x):
    return x * lax.rsqrt(jnp.mean(x * x, axis=-1, keepdims=True) + EPS)


def _rmsnorm_kernel(x_ref, g_ref, o_ref):
    o_ref[...] = (_rms(x_ref[...]) * g_ref[...]).astype(o_ref.dtype)


def _rmsnorm(x, g, out_dtype):
    m, d = x.shape
    tm = _tile(m, 512, 16)
    return pl.pallas_call(
        _rmsnorm_kernel,
        grid=(m // tm,),
        in_specs=[pl.BlockSpec((tm, d), lambda i: (i, 0)), pl.BlockSpec((1, d), lambda i: (0, 0))],
        out_specs=pl.BlockSpec((tm, d), lambda i: (i, 0)),
        out_shape=jax.ShapeDtypeStruct((m, d), out_dtype),
        compiler_params=_params("parallel"),
        name="rmsnorm",
    )(x, g.reshape(1, d))


def _matmul_kernel(a_ref, w_ref, o_ref):
    o_ref[...] = _dot(a_ref[...], w_ref[...]).astype(o_ref.dtype)


def _matmul(a, w, layer, out_dtype, *, col_off=0, n_cols=None, tm=2048, tn=512):
    m, k = a.shape
    n_cols = w.shape[2] - col_off if n_cols is None else n_cols
    tm = _tile(m, tm, 16)
    tn = _tile(n_cols, tn, LANE)
    assert col_off % tn == 0
    joff = col_off // tn
    return pl.pallas_call(
        _matmul_kernel,
        grid=(m // tm, n_cols // tn),
        in_specs=[pl.BlockSpec((tm, k), lambda i, j: (i, 0)),
                  pl.BlockSpec((None, k, tn), lambda i, j: (layer, 0, j + joff))],
        out_specs=pl.BlockSpec((tm, tn), lambda i, j: (i, j)),
        out_shape=jax.ShapeDtypeStruct((m, n_cols), out_dtype),
        compiler_params=_params("parallel", "arbitrary"),
        name="matmul",
    )(a, w)


def _matmul_side_kernel(a_ref, w_ref, ws_ref, o_ref, os_ref):
    a = a_ref[...]
    o_ref[...] = _dot(a, w_ref[...])

    @pl.when(pl.program_id(1) == 0)
    def _():
        os_ref[...] = _dot(a, ws_ref[...])


def _matmul_side(a, w, ws, layer, *, tm=2048, tn=512):
    m, k = a.shape
    n, ns = w.shape[2], ws.shape[2]
    tm = _tile(m, tm, 16)
    tn = _tile(n, tn, LANE)
    return pl.pallas_call(
        _matmul_side_kernel,
        grid=(m // tm, n // tn),
        in_specs=[pl.BlockSpec((tm, k), lambda i, j: (i, 0)),
                  pl.BlockSpec((None, k, tn), lambda i, j: (layer, 0, j)),
                  pl.BlockSpec((None, k, ns), lambda i, j: (layer, 0, 0))],
        out_specs=[pl.BlockSpec((tm, tn), lambda i, j: (i, j)),
                   pl.BlockSpec((tm, ns), lambda i, j: (i, 0))],
        out_shape=[jax.ShapeDtypeStruct((m, n), F32), jax.ShapeDtypeStruct((m, ns), F32)],
        compiler_params=_params("parallel", "arbitrary"),
        name="matmul_side",
    )(a, w, ws)


def _mm_res_norm_kernel(a_ref, w_ref, x_ref, g_ref, *out_refs):
    x = x_ref[...] + _dot(a_ref[...], w_ref[...])
    ho_ref = out_refs[-1]
    if len(out_refs) == 2:
        out_refs[0][...] = x
    ho_ref[...] = (_rms(x) * g_ref[...]).astype(ho_ref.dtype)


def _mm_res_norm(a, w, layer, x, g, norm_dtype, *, tm, keep_x=True):
    m, k = a.shape
    d = w.shape[2]
    tm = _tile(m, tm, 16)
    row_spec = pl.BlockSpec((tm, d), lambda i: (i, 0))
    outs = pl.pallas_call(
        _mm_res_norm_kernel,
        grid=(m // tm,),
        in_specs=[pl.BlockSpec((tm, k), lambda i: (i, 0)),
                  pl.BlockSpec((None, k, d), lambda i: (layer, 0, 0), pipeline_mode=pl.Buffered(1)),
                  row_spec,
                  pl.BlockSpec((1, d), lambda i: (0, 0))],
        out_specs=[row_spec] * (2 if keep_x else 1),
        out_shape=([jax.ShapeDtypeStruct((m, d), F32)] if keep_x else []) + [jax.ShapeDtypeStruct((m, d), norm_dtype)],
        compiler_params=_params("parallel"),
        name="mm_res_norm",
    )(a, w, x, g.reshape(1, d))
    return (outs[0], outs[1]) if keep_x else (None, outs[0])


def _ffn_in_kernel(h_ref, wg_ref, wu_ref, o_ref):
    h = h_ref[...]
    g = _dot(h, wg_ref[...])
    u = _dot(h, wu_ref[...])
    o_ref[...] = (_silu_tanh(g) * u).astype(o_ref.dtype)


def _ffn_in(h, w, layer, *, tm=1024, tn=512):
    m, d = h.shape
    dff = w.shape[2] // 2
    tm = _tile(m, tm, 16)
    tn = _tile(dff, tn, LANE)
    nj = dff // tn
    return pl.pallas_call(
        _ffn_in_kernel,
        grid=(m // tm, nj),
        in_specs=[pl.BlockSpec((tm, d), lambda i, j: (i, 0)),
                  pl.BlockSpec((None, d, tn), lambda i, j: (layer, 0, j)),
                  pl.BlockSpec((None, d, tn), lambda i, j: (layer, 0, j + nj))],
        out_specs=pl.BlockSpec((tm, tn), lambda i, j: (i, j)),
        out_shape=jax.ShapeDtypeStruct((m, dff), BF16),
        compiler_params=_params("parallel", "arbitrary"),
        name="ffn_in",
    )(h, w, w)


def _mix_kernel(ya_ref, yb_ref, ym_ref, wa_ref, wb_ref, wm_ref, ga_ref, gb_ref, gm_ref, o_ref):
    def gate(ref):
        return 0.5 + 0.5 * jnp.tanh(0.5 * ref[...].astype(F32))

    acc = gate(ga_ref) * _dot(ya_ref[...], wa_ref[...])
    acc = acc + gate(gb_ref) * _dot(yb_ref[...], wb_ref[...])
    acc = acc + gate(gm_ref) * _dot(ym_ref[...], wm_ref[...])
    o_ref[...] = acc.astype(o_ref.dtype)


def _mix(ya, yb, ym, wa, wb, wm, layer, p_f, gate_off, d, *, tm=1024, tn=512):
    m = ya.shape[0]
    tm = _tile(m, tm, 16)
    tn = _tile(d, tn, LANE)
    assert gate_off % tn == 0
    goff = gate_off // tn
    nj = d // tn

    def y_spec(y):
        return pl.BlockSpec((tm, y.shape[1]), lambda i, j: (i, 0))

    def w_spec(w):
        return pl.BlockSpec((None, w.shape[1], tn), lambda i, j: (layer, 0, j))

    def g_spec(which):
        return pl.BlockSpec((tm, tn), lambda i, j: (i, goff + which * nj + j))

    return pl.pallas_call(
        _mix_kernel,
        grid=(m // tm, nj),
        in_specs=[y_spec(ya), y_spec(yb), y_spec(ym), w_spec(wa), w_spec(wb), w_spec(wm),
                  g_spec(0), g_spec(1), g_spec(2)],
        out_specs=pl.BlockSpec((tm, tn), lambda i, j: (i, j)),
        out_shape=jax.ShapeDtypeStruct((m, d), BF16),
        compiler_params=_params("parallel", "arbitrary"),
        name="mix",
    )(ya, yb, ym, wa, wb, wm, p_f, p_f, p_f)


def _memattn_kernel(q_ref, k_ref, v_ref, o_ref, k16, v16, *dma, heads, hd, layer, nb):
    b = pl.program_id(0)

    if dma:
        kbuf, vbuf, sem = dma

        def copies(bb, slot):
            return [pltpu.make_async_copy(src.at[layer, bb, :, h, :], dst.at[slot, h], sem.at[slot, i * heads + h])
                    for i, (src, dst) in enumerate(((k_ref, kbuf), (v_ref, vbuf))) for h in range(heads)]

    @pl.when(pl.program_id(1) == 0)
    def _():
        if dma:
            @pl.when(b == 0)
            def _():
                for c in copies(0, 0):
                    c.start()

            @pl.when(b + 1 < nb)
            def _():
                for c in copies(b + 1, (b + 1) % 2):
                    c.start()

            for c in copies(b, b % 2):
                c.wait()
        for h in range(heads):
            if dma:
                k16[h] = kbuf[b % 2, h].astype(BF16)
                v16[h] = vbuf[b % 2, h].astype(BF16)
            else:
                k16[h] = k_ref[:, h * hd:(h + 1) * hd].astype(BF16)
                v16[h] = v_ref[:, h * hd:(h + 1) * hd].astype(BF16)

    for h in range(heads):
        sl = slice(h * hd, (h + 1) * hd)
        s = lax.dot_general(q_ref[:, sl], k16[h], _NT, preferred_element_type=F32) * (hd ** -0.5)
        p = jnp.exp(s - jnp.max(s, axis=-1, keepdims=True))
        p = p * (1.0 / jnp.sum(p, axis=-1, keepdims=True))
        o_ref[:, sl] = _dot(p.astype(BF16), v16[h]).astype(o_ref.dtype)


def _memattn(p_bf, q_off, memk, memv, layer, batch, t, *, tq=512):
    n_mem = memk.shape[2]
    heads, hd = M_HEADS, M_HD
    w = heads * hd
    tq = _tile(t, tq, 16)
    nt = t // tq
    assert q_off % w == 0
    qoff = q_off // w
    scratch = [pltpu.VMEM((heads, n_mem, hd), BF16), pltpu.VMEM((heads, n_mem, hd), BF16)]
    if memk.ndim == 5:
        mem_spec = pl.BlockSpec(memory_space=pl.ANY)
        scratch += [pltpu.VMEM((2, heads, n_mem, hd), F32), pltpu.VMEM((2, heads, n_mem, hd), F32),
                    pltpu.SemaphoreType.DMA((2, 2 * heads))]
        sem = ("arbitrary", "arbitrary")
    else:
        mem_spec = pl.BlockSpec((None, None, n_mem, w), lambda b, i: (layer, b, 0, 0))
        sem = ("parallel", "arbitrary")
    return pl.pallas_call(
        functools.partial(_memattn_kernel, heads=heads, hd=hd, layer=layer, nb=batch),
        grid=(batch, nt),
        in_specs=[pl.BlockSpec((tq, w), lambda b, i: (b * nt + i, qoff)), mem_spec, mem_spec],
        out_specs=pl.BlockSpec((tq, w), lambda b, i: (b * nt + i, 0)),
        out_shape=jax.ShapeDtypeStruct((batch * t, w), BF16),
        scratch_shapes=scratch,
        compiler_params=_params(*sem),
        name="memattn",
    )(p_bf, memk, memv)


def _split3(x):
    hi = x.astype(BF16)
    r = x - hi.astype(F32)
    mid = r.astype(BF16)
    lo = (r - mid.astype(F32)).astype(BF16)
    return hi, mid, lo


def _mlstm_kernel(*refs, L, nc, heads, dqk, dv, has_state):
    refs = refs[1:]
    if has_state:
        (qk_ref, v_ref, og_ref, ifc_ref, ifr_ref, bc_ref, br_ref, cw_ref, cb_ref, g_ref,
         c0_ref, n0_ref, m0_ref, conv0_ref,
         y_ref, c_out, n_out, m_out, conv_out, ubuf, caug, m_scr, corr_scr) = refs
    else:
        (qk_ref, v_ref, og_ref, ifc_ref, ifr_ref, bc_ref, br_ref, cw_ref, cb_ref, g_ref,
         y_ref, c_out, n_out, m_out, conv_out, ubuf, caug, m_scr, corr_scr) = refs
    c_idx = pl.program_id(1)
    nq = heads * dqk

    @pl.when(c_idx == 0)
    def _init():
        if has_state:
            ubuf[...] = conv0_ref[...]
            caug[:, :, :dv] = c0_ref[...]
            caug[:, :, dv:] = n0_ref[...]
            m_scr[...] = m0_ref[...]
        else:
            ubuf[...] = jnp.zeros((SUBLANE, 2 * nq), F32)
            caug[...] = jnp.zeros(caug.shape, F32)
            m_scr[...] = jnp.zeros(m_scr.shape, F32)

    pre_c = ifc_ref[...] + bc_ref[...]
    pre_r = ifr_ref[...] + br_ref[...]
    lf_c = _log_sigmoid(pre_c)
    lf_r = _log_sigmoid(pre_r)
    row = lax.broadcasted_iota(jnp.int32, (L, L), 0)
    col = lax.broadcasted_iota(jnp.int32, (L, L), 1)
    causal = row >= col
    tri = jnp.where(causal, 1.0, 0.0).astype(BF16)
    tri_t = jnp.where(row <= col, 1.0, 0.0).astype(BF16)
    fcum_c = sum(_dot(tri, part) for part in _split3(lf_c))
    fcum_r = sum(_dot(part, tri_t) for part in _split3(lf_r))

    shift = jnp.concatenate(
        [jnp.where(row - col == CONV_W - 1 - j, 1.0, 0.0).astype(BF16) for j in range(CONV_W - 1)], axis=0)
    sub = lax.broadcasted_iota(jnp.int32, (SUBLANE, 2 * nq), 0)
    corr = jnp.zeros((SUBLANE, 2 * nq), F32)
    for t0 in range(CONV_W - 1):
        r = sum(cw_ref[j:j + 1, :] * ubuf[SUBLANE - (CONV_W - 1) + t0 + j:SUBLANE - (CONV_W - 2) + t0 + j, :]
                for j in range(CONV_W - 1 - t0))
        corr = jnp.where(sub == t0, r, corr)
    corr_scr[...] = corr

    ones_blk = jnp.where(lax.broadcasted_iota(jnp.int32, (L, LANE), 1) == 0, 1.0, 0.0).astype(BF16)

    def conv(c0, scale):
        sh = _dot(shift, qk_ref[:, c0:c0 + dqk])
        blocks = []
        for b0 in range(0, dqk, LANE):
            cs = slice(c0 + b0, c0 + b0 + LANE)
            acc = cb_ref[:, cs] + cw_ref[CONV_W - 1:CONV_W, cs] * qk_ref[:, cs].astype(F32)
            for j in range(CONV_W - 1):
                acc = acc + cw_ref[j:j + 1, cs] * sh[j * L:(j + 1) * L, b0:b0 + LANE]
            acc = jnp.concatenate([acc[:SUBLANE] + corr_scr[:, cs], acc[SUBLANE:]], axis=0)
            act = _silu_tanh(acc)
            blocks.append((act if scale is None else act * scale).astype(BF16))
        return jnp.concatenate(blocks, axis=1)

    def head_group(hs):
        q16 = {h: conv(h * dqk, None) for h in hs}
        k16 = {h: conv(nq + h * dqk, dqk ** -0.5) for h in hs}
        qc = {h: _dot(q16[h], caug[h].astype(BF16)) for h in hs}
        s_raw = {h: lax.dot_general(q16[h], k16[h], _NT, preferred_element_type=F32) for h in hs}

        m_t, w_state, w_end, decay, s16 = {}, {}, {}, {}, {}
        for h in hs:
            fc_c = fcum_c[:, heads + h:heads + h + 1]
            fc_r = fcum_r[heads + h:heads + h + 1, :]
            li_c = pre_c[:, h:h + 1]
            li_r = pre_r[h:h + 1, :]
            m_prev = m_scr[h, 0:1, 0:1]
            d = jnp.where(causal, fc_c - fc_r + li_r, NEG_BIG)
            inter = fc_c + m_prev
            m_t[h] = jnp.maximum(inter, jnp.max(d, axis=1, keepdims=True))
            w_state[h] = jnp.exp(inter - m_t[h])
            s16[h] = (s_raw[h] * jnp.exp(d - m_t[h])).astype(BF16)
            m_new = m_t[h][L - 1:L, :]
            fc_last = fc_c[L - 1:L, :]
            decay[h] = jnp.exp(fc_last + m_prev - m_new)
            w_end[h] = jnp.exp(fc_last - fc_c + li_c - m_new)
            m_scr[h] = jnp.broadcast_to(m_new, (SUBLANE, LANE))

        def v_of(h):
            return v_ref[:, h * dv:(h + 1) * dv]

        num = {h: _dot(s16[h], v_of(h)) for h in hs}
        dsum = {h: _dot(s16[h], ones_blk) for h in hs}
        kw = {h: k16[h] * w_end[h].astype(BF16) for h in hs}

        for h in hs:
            den = dsum[h][:, 0:1] + w_state[h] * qc[h][:, dv:dv + 1]
            rden = 1.0 / jnp.maximum(jnp.abs(den), jnp.exp(-m_t[h]))
            hb, ssq = [], 0.0
            for b0 in range(0, dv, LANE):
                blk = (num[h][:, b0:b0 + LANE] + w_state[h] * qc[h][:, b0:b0 + LANE]) * rden
                ssq = ssq + jnp.sum(blk * blk, axis=1, keepdims=True)
                hb.append(blk)
            rinv = lax.rsqrt(ssq * (1.0 / dv) + EPS)
            for i, b0 in enumerate(range(0, dv, LANE)):
                vs = slice(h * dv + b0, h * dv + b0 + LANE)
                gate = 0.5 + 0.5 * jnp.tanh(0.5 * og_ref[:, vs].astype(F32))
                y_ref[:, vs] = (hb[i] * rinv * g_ref[:, vs] * gate).astype(y_ref.dtype)

        for h in hs:
            caug[h, :, :dv] = decay[h] * caug[h, :, :dv] + lax.dot_general(
                kw[h], v_of(h), _TN, preferred_element_type=F32)
            caug[h, :, dv:] = decay[h] * caug[h, :, dv:] + lax.dot_general(
                kw[h], ones_blk, _TN, preferred_element_type=F32)

    group = max(1, min(heads, MLSTM_GROUP_ROWS // L))
    for h0 in range(0, heads, group):
        head_group(list(range(h0, min(h0 + group, heads))))

    ubuf[...] = qk_ref[L - 2 * SUBLANE:L, :].astype(F32)[SUBLANE:, :]

    @pl.when(c_idx == nc - 1)
    def _fin():
        c_out[...] = caug[:, :, :dv]
        n_out[...] = caug[:, :, dv:]
        m_out[...] = m_scr[...]
        conv_out[...] = ubuf[...]


def _mlstm(p_f, qk_off, og_off, p_bf, v_off, p_if, b_i, b_f, conv_w, conv_b, g, state, c_acc, layer, batch, t):
    heads, dqk, dv = A_HEADS, A_DQK, A_DV
    nq, nv = heads * dqk, heads * dv
    m_rows = batch * t
    L = min(MLSTM_CHUNK, t)
    nc = t // L
    assert t % L == 0 and L >= 2 * SUBLANE and qk_off % (2 * nq) == 0 and og_off % nv == 0 and v_off % nv == 0
    has_state = state is not None

    ifr = p_if[:, :SUBLANE].reshape(m_rows // L, L, SUBLANE).transpose(0, 2, 1)
    bias = jnp.concatenate([b_i, b_f]).astype(F32)
    bias_c = jnp.zeros((1, LANE), F32).at[0, :2 * heads].set(bias)
    bias_r = jnp.zeros((SUBLANE, 1), F32).at[:2 * heads, 0].set(bias)

    def rows(width, off):
        return pl.BlockSpec((L, width), lambda b, c: (b * nc + c, off // width))

    def whole(shape):
        return pl.BlockSpec(shape, lambda b, c: (0,) * len(shape))

    in_specs = [rows(2 * nq, qk_off), rows(nv, v_off), rows(nv, og_off), rows(LANE, 0),
                pl.BlockSpec((None, SUBLANE, L), lambda b, c: (b * nc + c, 0, 0)),
                whole((1, LANE)), whole((SUBLANE, 1)), whole((CONV_W, 2 * nq)), whole((1, 2 * nq)),
                whole((1, nv))]
    args = [p_f, p_bf, p_f, p_if, ifr, bias_c, bias_r, conv_w, conv_b.reshape(1, 2 * nq), g.reshape(1, nv)]
    if has_state:
        c0, n0, m0, conv0 = state
        n0p = jnp.pad(n0[layer][..., None], ((0, 0), (0, 0), (0, 0), (0, LANE - 1)))
        m0p = jnp.broadcast_to(m0[layer][:, :, None, None], (batch, heads, SUBLANE, LANE))
        conv0p = jnp.pad(conv0[layer], ((0, 0), (SUBLANE - (CONV_W - 1), 0), (0, 0)))
        in_specs += [pl.BlockSpec((None, None, heads, dqk, dv), lambda b, c: (layer, b, 0, 0, 0)),
                     pl.BlockSpec((None, heads, dqk, LANE), lambda b, c: (b, 0, 0, 0)),
                     pl.BlockSpec((None, heads, SUBLANE, LANE), lambda b, c: (b, 0, 0, 0)),
                     pl.BlockSpec((None, SUBLANE, 2 * nq), lambda b, c: (b, 0, 0))]
        args += [c0, n0p, m0p, conv0p]

    if c_acc is None:
        c_acc = jnp.zeros((DEPTH, batch, heads, dqk, dv), F32)
    in_specs = [pl.BlockSpec(memory_space=pl.ANY)] + in_specs
    args = [c_acc] + args
    aliases = {0: 1}
    out_specs = [pl.BlockSpec((L, nv), lambda b, c: (b * nc + c, 0)),
                 pl.BlockSpec((None, None, heads, dqk, dv), lambda b, c: (layer, b, 0, 0, 0)),
                 pl.BlockSpec((None, heads, dqk, LANE), lambda b, c: (b, 0, 0, 0)),
                 pl.BlockSpec((None, heads, SUBLANE, LANE), lambda b, c: (b, 0, 0, 0)),
                 pl.BlockSpec((None, SUBLANE, 2 * nq), lambda b, c: (b, 0, 0))]
    out_shape = [jax.ShapeDtypeStruct((m_rows, nv), BF16),
                 jax.ShapeDtypeStruct((DEPTH, batch, heads, dqk, dv), F32),
                 jax.ShapeDtypeStruct((batch, heads, dqk, LANE), F32),
                 jax.ShapeDtypeStruct((batch, heads, SUBLANE, LANE), F32),
                 jax.ShapeDtypeStruct((batch, SUBLANE, 2 * nq), F32)]
    ya, c_new, n_new, m_new, conv_new = pl.pallas_call(
        functools.partial(_mlstm_kernel, L=L, nc=nc, heads=heads, dqk=dqk, dv=dv, has_state=has_state),
        grid=(batch, nc),
        in_specs=in_specs,
        out_specs=out_specs,
        out_shape=out_shape,
        input_output_aliases=aliases,
        scratch_shapes=[pltpu.VMEM((SUBLANE, 2 * nq), F32),
                        pltpu.VMEM((heads, dqk, dv + LANE), F32),
                        pltpu.VMEM((heads, SUBLANE, LANE), F32),
                        pltpu.VMEM((SUBLANE, 2 * nq), F32)],
        compiler_params=_params("parallel", "arbitrary"),
        name="mlstm",
    )(*args)
    return ya, c_new, n_new[..., 0], m_new[:, :, 0, 0], conv_new[:, SUBLANE - (CONV_W - 1):, :]


def _neg_abs(x):
    bits = lax.bitcast_convert_type(x, jnp.uint32) | jnp.uint32(0x80000000)
    return lax.bitcast_convert_type(bits, F32)


def _ref_rows(g_scr, L, bs, n):
    hs = bs // 2

    def bcast(r, rows):
        return jnp.broadcast_to(g_scr[r:r + 1, :], (rows, n))

    if bs >= 2 * SUBLANE:
        return jnp.concatenate([bcast(b0 + hs - 1, bs) for b0 in range(0, L, bs)], axis=0)
    sub = lax.broadcasted_iota(jnp.int32, (SUBLANE, n), 0)
    groups = []
    for g0 in range(0, L, SUBLANE):
        r0 = g0 + hs - 1
        val = bcast(r0, SUBLANE)
        for b0 in range(bs, SUBLANE, bs):
            val = jnp.where(sub >= b0, bcast(r0 + b0, SUBLANE), val)
        groups.append(val)
    return jnp.concatenate(groups, axis=0)


def _hgrn_kernel(*refs, L, nsub, nc, heads, dk, dv, layer, has_state):
    refs = refs[1:]
    if has_state:
        q_ref, f_ref, v_ref, gt_ref, lbraw_ref, g_ref, s0_ref, y_ref, s_out, st, lb_scr, g_scr = refs
    else:
        q_ref, f_ref, v_ref, gt_ref, lbraw_ref, g_ref, y_ref, s_out, st, lb_scr, g_scr = refs
    c_idx = pl.program_id(1)

    @pl.when(c_idx == 0)
    def _init():
        for h in range(heads):
            st[h] = s0_ref[h].T if has_state else jnp.zeros((dv, dk), F32)
        raw = lbraw_ref[...]
        e = jnp.exp(raw - jnp.max(raw, axis=0, keepdims=True))
        sm = e / jnp.sum(e, axis=0, keepdims=True)
        lb = jnp.zeros((1, heads * dk), F32)
        for j in range(1, layer + 1):
            lb = lb + sm[j:j + 1, :]
        lb_scr[...] = lb

    row = lax.broadcasted_iota(jnp.int32, (L, L), 0)
    col = lax.broadcasted_iota(jnp.int32, (L, L), 1)
    tri = jnp.where(row >= col, 1.0, 0.0).astype(BF16)
    diag = row == col
    levels = []
    bs = 2
    while bs <= L:
        hs, lg = bs // 2, bs.bit_length() - 1
        pair = ((row >> lg) == (col >> lg)) & ((row & (bs - 1)) >= hs) & ((col & (bs - 1)) < hs)
        levels.append((bs, pair))
        bs *= 2

    def nt(a, b):
        return lax.dot_general(a, b, _NT, preferred_element_type=F32)

    def head_group(hs, sub):
        rs = pl.ds(sub * L, L)
        kb16, qb16, gs, f16 = [], [], [], []
        for h in hs:
            kc = pl.ds(h * dk, dk)
            z = f_ref[rs, kc]
            lb = lb_scr[:, kc]
            t = jnp.exp(_neg_abs(z))
            r = 1.0 / (1.0 + t)
            pos = z >= 0.0
            f = jnp.where(pos, 1.0 + lb * t, lb + t) * r
            kb = (1.0 - lb) * (jnp.where(pos, t, 1.0) * r)
            qb = _silu_tanh(q_ref[rs, kc].astype(F32))
            kb16.append(kb.astype(BF16))
            qb16.append(qb.astype(BF16))
            f16.append(f.astype(BF16))
            gs.append(_dot(tri, jnp.concatenate(_split3(jnp.log(f) * LOG2E)[:2], axis=1)))
        g2, pend = [], []
        for u, h in enumerate(hs):
            g2.append(gs[u][:, :dk] + gs[u][:, dk:])
            g_scr[sub * heads + h] = g2[u]
            pend.append(nt(qb16[u], kb16[u]))
        a = [0.0] * len(hs)
        mask = diag
        for bs, pair in levels:
            new = []
            for u, h in enumerate(hs):
                if bs == 2:
                    new.append(nt(qb16[u] * f16[u], kb16[u]))
                    continue
                e = jnp.exp2(_neg_abs(g2[u] - _ref_rows(g_scr.at[sub * heads + h], L, bs, dk))).astype(BF16)
                new.append(nt(qb16[u] * e, kb16[u] * e))
            a = [jnp.where(mask, p, x) for p, x in zip(pend, a)]
            pend, mask = new, pair
        a = [jnp.where(mask, p, x) for p, x in zip(pend, a)]
        o = []
        for u, h in enumerate(hs):
            vc = pl.ds(h * dv, dv)
            qg = qb16[u] * jnp.exp2(g2[u]).astype(BF16)
            o.append((_dot(a[u].astype(BF16), v_ref[rs, vc]), nt(qg, st[h].astype(BF16))))
        upd = []
        for u, h in enumerate(hs):
            g_end = g2[u][L - 1:L, :]
            ke = kb16[u] * jnp.exp2(g_end - g2[u]).astype(BF16)
            upd.append(lax.dot_general(v_ref[rs, pl.ds(h * dv, dv)], ke, _TN, preferred_element_type=F32))
        for u, h in enumerate(hs):
            vc = pl.ds(h * dv, dv)
            y = _rms(o[u][0] + o[u][1]) * g_ref[...] * _silu_tanh(gt_ref[rs, vc].astype(F32))
            y_ref[rs, vc] = y.astype(y_ref.dtype)
        for u, h in enumerate(hs):
            st[h] = jnp.exp2(g2[u][L - 1:L, :]) * st[h] + upd[u]

    for sub in range(nsub):
        for h0 in range(0, heads, HGRN_UNROLL):
            head_group(list(range(h0, h0 + HGRN_UNROLL)), sub)

    @pl.when(c_idx == nc - 1)
    def _fin():
        for h in range(heads):
            s_out[h] = st[h].T


def _hgrn(p_f, q_off, f_off, gt_off, p_bf, v_off, lb_raw, g, s0, s_acc, layer, batch, t):
    heads, dk, dv = B_HEADS, B_DK, B_DV
    nk, nv = heads * dk, heads * dv
    L = min(HGRN_CHUNK, t)
    nsub = HGRN_SUB if t % (L * HGRN_SUB) == 0 else 1
    nc = t // (L * nsub)
    assert t % L == 0 and q_off % nk == 0 and f_off % nk == 0 and gt_off % nv == 0 and v_off % nv == 0
    assert heads % HGRN_UNROLL == 0
    has_state = s0 is not None

    def rows(width, off):
        return pl.BlockSpec((L * nsub, width), lambda b, c: (b * nc + c, off // width))

    in_specs = [rows(nk, q_off), rows(nk, f_off), rows(nv, v_off), rows(nv, gt_off),
                pl.BlockSpec(lb_raw.shape, lambda b, c: (0, 0)),
                pl.BlockSpec((1, dv), lambda b, c: (0, 0))]
    args = [p_bf, p_f, p_bf, p_bf, lb_raw, g.reshape(1, dv)]
    if has_state:
        in_specs.append(pl.BlockSpec((None, None, heads, dk, dv), lambda b, c: (layer, b, 0, 0, 0)))
        args.append(s0)
    if s_acc is None:
        s_acc = jnp.zeros((DEPTH, batch, heads, dk, dv), F32)
    in_specs = [pl.BlockSpec(memory_space=pl.ANY)] + in_specs
    args = [s_acc] + args
    aliases = {0: 1}
    return pl.pallas_call(
        functools.partial(_hgrn_kernel, L=L, nsub=nsub, nc=nc, heads=heads, dk=dk, dv=dv, layer=layer,
                          has_state=has_state),
        grid=(batch, nc),
        in_specs=in_specs,
        out_specs=[pl.BlockSpec((L * nsub, nv), lambda b, c: (b * nc + c, 0)),
                   pl.BlockSpec((None, None, heads, dk, dv), lambda b, c: (layer, b, 0, 0, 0))],
        out_shape=[jax.ShapeDtypeStruct((batch * t, nv), BF16),
                   jax.ShapeDtypeStruct((DEPTH, batch, heads, dk, dv), F32)],
        input_output_aliases=aliases,
        scratch_shapes=[pltpu.VMEM((heads, dv, dk), F32),
                        pltpu.VMEM((1, nk), F32),
                        pltpu.VMEM((nsub * heads, L, dk), F32)],
        compiler_params=_params("parallel", "arbitrary"),
        name="hgrn",
    )(*args)


def _regroup_kernel(wt_ref, o_ref):
    o_ref[...] = wt_ref[0].T.astype(BF16)


def _regroup(wt, sections, tn):
    depth, _, k = wt.shape
    plan, o = [], 0
    for s, width in sections:
        assert width % tn == 0
        plan.append((o // tn, (o + width) // tn, s))
        o += width
    aligned = all(s % SUBLANE == 0 for s, _ in sections)

    def src_row(j):
        r = 0
        for lo, hi, s in plan:
            r = jnp.where((j >= lo) & (j < hi), s + (j - lo) * tn, r)
        return pl.multiple_of(r, SUBLANE) if aligned else r

    return pl.pallas_call(
        _regroup_kernel,
        grid=(depth, o // tn),
        in_specs=[pl.BlockSpec((pl.Element(1), pl.Element(tn), pl.Element(k)),
                               lambda l, j: (l, src_row(j), 0))],
        out_specs=pl.BlockSpec((None, k, tn), lambda l, j: (l, 0, j)),
        out_shape=jax.ShapeDtypeStruct((depth, k, o), BF16),
        compiler_params=_params("parallel", "parallel"),
        name="regroup_w_in",
    )(wt)


def _prep_weights(w_in, w_proj_a, w_proj_b, w_proj_m, w_out, w_ffn_in, w_ffn_out, w_mem_kv):
    a_qk = 2 * A_HEADS * A_DQK
    a_v = A_HEADS * A_DV
    b_k = B_HEADS * B_DK
    b_v = B_HEADS * B_DV
    m_w = M_HEADS * M_HD
    depth, k, n_in = w_in.shape
    d = w_out.shape[1]
    src, pos = {}, 0
    for name, width in (("qk_a", a_qk), ("v_a", a_v), ("o_a", a_v), ("if_a", 2 * A_HEADS), ("q_b", b_k),
                        ("f_b", b_k), ("i_b", b_v), ("gt_b", b_v), ("q_m", m_w), ("gates", 3 * d)):
        src[name] = (pos, width)
        pos += width
    assert pos == n_in
    order = ("v_a", "i_b", "q_m", "qk_a", "o_a", "q_b", "gt_b", "gates")
    offs, n_bf = {}, 0
    for name in order:
        offs[name] = n_bf
        n_bf += src[name][1]
    wt = jnp.swapaxes(w_in, 1, 2)
    tn = 512
    while any(src[name][1] % tn for name in order + ("f_b",)):
        tn //= 2
    w_bf = _regroup(wt, [src[name] for name in order], tn)
    w_f = _regroup(wt, [src["f_b"]], tn)
    s_if, n_if = src["if_a"]
    w_if = jnp.pad(w_in[:, :, s_if:s_if + n_if], ((0, 0), (0, 0), (0, LANE - n_if))).astype(BF16)
    return dict(w_bf=w_bf, w_f=w_f, w_if=w_if, offs=offs,
                w_proj_a=w_proj_a.astype(BF16), w_proj_b=w_proj_b.astype(BF16),
                w_proj_m=w_proj_m.astype(BF16), w_out=w_out.astype(BF16),
                w_ffn_in=w_ffn_in.astype(BF16), w_ffn_out=w_ffn_out.astype(BF16),
                w_mem_kv=w_mem_kv.astype(BF16))


def _trunk(x, memk, memv, state, W, P):
    batch, t, d = x.shape
    offs = W["offs"]
    x2 = x.reshape(batch * t, d)
    h = _rmsnorm(x2, P["g_mix"][0], BF16)
    new_conv, new_n, new_m = [], [], []
    ca = sb = None
    for l in range(DEPTH):
        p_bf = _matmul(h, W["w_bf"], l, BF16, tn=1024)
        p_f, p_if = _matmul_side(h, W["w_f"], W["w_if"], l)

        ya, ca, na, ma, cbuf = _mlstm(
            p_bf, offs["qk_a"], offs["o_a"], p_bf, offs["v_a"], p_if, P["b_igate"][l], P["b_fgate"][l],
            P["conv_w"][l], P["conv_b"][l], P["g_mlstm"][l],
            None if state is None else state[:4], ca, l, batch, t)
        yb, sb = _hgrn(p_f, offs["q_b"], 0, offs["gt_b"], p_bf, offs["i_b"], P["lb_raw"],
                       P["g_hgrn"][l], None if state is None else state[4], sb, l, batch, t)
        ym = _memattn(p_bf, offs["q_m"], memk, memv, l, batch, t)

        mix = _mix(ya, yb, ym, W["w_proj_a"], W["w_proj_b"], W["w_proj_m"], l, p_bf, offs["gates"], d)
        x2, h2 = _mm_res_norm(mix, W["w_out"], l, x2, P["g_ffn"][l], BF16, tm=512)
        act = _ffn_in(h2, W["w_ffn_in"], l)
        last = l == DEPTH - 1
        x2, h = _mm_res_norm(act, W["w_ffn_out"], l, x2, P["g_final"] if last else P["g_mix"][l + 1],
                             F32 if last else BF16, tm=256, keep_x=not last)
        new_conv.append(cbuf)
        new_n.append(na)
        new_m.append(ma)
    return (h.reshape(batch, t, d), jnp.stack(new_conv), ca, jnp.stack(new_n), jnp.stack(new_m), sb)


def kernel(x_prompt, x_sample, mem_prompt, cache_mem_k, cache_mem_v, state_mlstm_c, state_mlstm_n, state_mlstm_m, state_mlstm_conv, state_hgrn_s, g_mix, w_in, conv_w, conv_b, b_igate, b_fgate, g_mlstm, lb_raw, g_hgrn, g_mem, w_mem_kv, w_proj_a, w_proj_b, w_proj_m, w_out, g_ffn, w_ffn_in, w_ffn_out, g_final):
    W = _prep_weights(w_in, w_proj_a, w_proj_b, w_proj_m, w_out, w_ffn_in, w_ffn_out, w_mem_kv)
    P = dict(g_mix=g_mix, conv_w=conv_w, conv_b=conv_b, b_igate=b_igate, b_fgate=b_fgate, g_mlstm=g_mlstm,
             lb_raw=lb_raw, g_hgrn=g_hgrn, g_ffn=g_ffn, g_final=g_final)
    batch, n_mem, d = mem_prompt.shape
    m_w = M_HEADS * M_HD

    mem2 = mem_prompt.reshape(batch * n_mem, d)
    ks, vs = [], []
    for l in range(DEPTH):
        mem_n = _rmsnorm(mem2, g_mem[l], BF16)
        ks.append(_matmul(mem_n, W["w_mem_kv"], l, F32, col_off=0, n_cols=m_w))
        vs.append(_matmul(mem_n, W["w_mem_kv"], l, F32, col_off=m_w, n_cols=m_w))
    mem_shape = (DEPTH, batch, n_mem, M_HEADS, M_HD)
    mem_k = jnp.stack(ks).reshape(DEPTH, batch, n_mem, m_w)
    mem_v = jnp.stack(vs).reshape(DEPTH, batch, n_mem, m_w)

    y_p, conv_p, c_p, n_p, m_p, s_p = _trunk(x_prompt, mem_k, mem_v, None, W, P)
    mem_k, mem_v = mem_k.reshape(mem_shape), mem_v.reshape(mem_shape)

    state = (state_mlstm_c, state_mlstm_n, state_mlstm_m, state_mlstm_conv, state_hgrn_s)
    y_s, conv_s, c_s, n_s, m_s, s_s = _trunk(x_sample, cache_mem_k, cache_mem_v, state, W, P)

    return (y_p, y_s, mem_k, mem_v, conv_p, c_p, n_p, m_p, s_p, conv_s, c_s, n_s, m_s, s_s)
```
